```python
import math
import jax, jax.numpy as jnp
from jax import lax
import numpy as np

D_MODEL = 2048
BATCH = 8
SEQ = 4096
DEPTH = 4
DEC_BATCH = 2
DEC_SEQ = 8192
PAST_LEN = 128

GRID_W = 64
PLE_DIM = 256
D_FF = 4 * D_MODEL
NORM_EPS = 1e-6
DN_K_HEADS = 16
DN_V_HEADS = 32
DN_K_DIM = 128
DN_V_DIM = 128
DN_KEY_WIDTH = DN_K_HEADS * DN_K_DIM
DN_VALUE_WIDTH = DN_V_HEADS * DN_V_DIM
DN_QKV_WIDTH = 2 * DN_KEY_WIDTH + DN_VALUE_WIDTH
DN_IN_WIDTH = DN_QKV_WIDTH + DN_VALUE_WIDTH + 4 * DN_V_HEADS
DN_CONV_W = 5
DN_CHUNK = 64
ATT_HEADS = 16
ATT_KV_HEADS = 8
ATT_HEAD_DIM = 128
ATT_GROUP = ATT_HEADS // ATT_KV_HEADS
ATT_Q_WIDTH = ATT_HEADS * ATT_HEAD_DIM
ATT_KV_WIDTH = ATT_KV_HEADS * ATT_HEAD_DIM
ATT_IN_WIDTH = ATT_Q_WIDTH + 2 * ATT_KV_WIDTH
ATT_Q_BLOCK = 128
ROPE_THETA = 10000.0
N_DN_LAYERS = (DEPTH + 1) // 2
N_ATT_LAYERS = DEPTH // 2

kernel_name = 'hybrid_deltanet_axial_gqa_encoder'


def rms_norm(x, gain):
    xf = x.astype(jnp.float32)
    y = xf * lax.rsqrt(jnp.mean(xf * xf, axis=-1, keepdims=True) + NORM_EPS)
    return (y * gain.astype(jnp.float32)).astype(x.dtype)


def l2_normalize(x):
    return x * lax.rsqrt(jnp.sum(x * x, axis=-1, keepdims=True) + NORM_EPS)


def axial_rope_tables(n):
    rows = n // GRID_W
    row = jnp.repeat(jnp.arange(rows, dtype=jnp.float32), GRID_W)
    col = jnp.tile(jnp.arange(GRID_W, dtype=jnp.float32), rows)
    half = ATT_HEAD_DIM // 2
    inv_freq = ROPE_THETA ** (-jnp.arange(0, half, 2, dtype=jnp.float32) / half)
    ang = jnp.concatenate([row[:, None] * inv_freq, col[:, None] * inv_freq], axis=-1)
    return jnp.cos(ang), jnp.sin(ang)


def apply_axial_rope(x, cos, sin):
    xp = x.reshape(x.shape[:-1] + (ATT_HEAD_DIM // 2, 2))
    x1, x2 = xp[..., 0], xp[..., 1]
    c = cos[None, :, None, :].astype(x.dtype)
    s = sin[None, :, None, :].astype(x.dtype)
    return jnp.stack([x1 * c - x2 * s, x1 * s + x2 * c], axis=-1).reshape(x.shape)


def centred_depthwise_conv(x, w):
    c = x.shape[-1]
    pad = DN_CONV_W // 2
    return lax.conv_general_dilated(x, w[:, None, :].astype(x.dtype), window_strides=(1,),
                                    padding=[(pad, pad)], dimension_numbers=('NWC', 'WIO', 'NWC'),
                                    feature_group_count=c)


def gated_delta_rule_chunked(q, k, v, g, beta):
    b, s, h, dk = q.shape
    dv = v.shape[-1]
    n = s // DN_CHUNK

    def to_chunks(t):
        t = t.reshape((b, n, DN_CHUNK, h) + t.shape[3:])
        return jnp.moveaxis(t, (1, 3), (0, 2))

    idx = jnp.arange(DN_CHUNK)
    incl = idx[:, None] >= idx[None, :]
    strict = idx[:, None] > idx[None, :]
    eye = jnp.eye(DN_CHUNK, dtype=jnp.float32)

    def step(state, xs):
        qc, kc, vc, gc, bc = xs
        gcum = jnp.cumsum(gc, axis=-1)
        decay = jnp.exp(jnp.where(incl, gcum[..., :, None] - gcum[..., None, :], -jnp.inf))
        kb = kc * bc[..., None]
        lower = jnp.where(strict, jnp.einsum('bhid,bhjd->bhij', kb, kc) * decay, 0.0)
        rhs = jnp.concatenate([vc * bc[..., None], kb * jnp.exp(gcum)[..., None]], axis=-1)
        sol = lax.linalg.triangular_solve(eye + lower, rhs, left_side=True, lower=True)
        u, w = sol[..., :dv], sol[..., dv:]
        v_new = u - jnp.einsum('bhck,bhkv->bhcv', w, state)
        intra = jnp.where(incl, jnp.einsum('bhid,bhjd->bhij', qc, kc) * decay, 0.0)
        o = (jnp.einsum('bhck,bhkv->bhcv', qc * jnp.exp(gcum)[..., None], state)
             + jnp.einsum('bhij,bhjv->bhiv', intra, v_new))
        g_last = gcum[..., -1:]
        k_dec = kc * jnp.exp(g_last - gcum)[..., None]
        state = state * jnp.exp(g_last)[..., None] + jnp.einsum('bhck,bhcv->bhkv', k_dec, v_new)
        return state, o

    state0 = jnp.zeros((b, h, dk, dv), jnp.float32)
    _, o = lax.scan(step, state0, (to_chunks(q), to_chunks(k), to_chunks(v), to_chunks(g), to_chunks(beta)))
    return jnp.moveaxis(o, (0, 2), (1, 3)).reshape(b, s, h, dv)


def deltanet_mixer(xn, w_in, conv_w, a_log, dt_bias, out_gain, w_out):
    b, s, _ = xn.shape
    proj = xn @ w_in
    qkv = jax.nn.silu(centred_depthwise_conv(proj[..., :DN_QKV_WIDTH], conv_w))
    z = proj[..., DN_QKV_WIDTH:DN_QKV_WIDTH + DN_VALUE_WIDTH].reshape(b, s, DN_V_HEADS, DN_V_DIM)
    ba = proj[..., DN_QKV_WIDTH + DN_VALUE_WIDTH:].astype(jnp.float32).reshape(b, s, 2, 2, DN_V_HEADS)
    rep = DN_V_HEADS // DN_K_HEADS
    q = qkv[..., :DN_KEY_WIDTH].astype(jnp.float32).reshape(b, s, DN_K_HEADS, DN_K_DIM)
    k = qkv[..., DN_KEY_WIDTH:2 * DN_KEY_WIDTH].astype(jnp.float32).reshape(b, s, DN_K_HEADS, DN_K_DIM)
    v = qkv[..., 2 * DN_KEY_WIDTH:].astype(jnp.float32).reshape(b, s, DN_V_HEADS, DN_V_DIM)
    q = jnp.repeat(l2_normalize(q), rep, axis=2) * (DN_K_DIM ** -0.5)
    k = jnp.repeat(l2_normalize(k), rep, axis=2)
    beta = jax.nn.sigmoid(ba[:, :, 0])
    g = -jnp.exp(a_log.astype(jnp.float32)) * jax.nn.softplus(ba[:, :, 1] + dt_bias.astype(jnp.float32))
    o_fwd = gated_delta_rule_chunked(q, k, v, g[:, :, 0], beta[:, :, 0])
    flip = lambda t: jnp.flip(t, axis=1)
    o_bwd = flip(gated_delta_rule_chunked(flip(q), flip(k), flip(v), flip(g[:, :, 1]), flip(beta[:, :, 1])))
    o = rms_norm(o_fwd + o_bwd, out_gain) * jax.nn.silu(z.astype(jnp.float32))
    return o.reshape(b, s, DN_VALUE_WIDTH).astype(xn.dtype) @ w_out


def attention_mixer(xn, w_in, q_gain, k_gain, w_out, cos, sin):
    b, s, _ = xn.shape
    proj = xn @ w_in
    q = proj[..., :ATT_Q_WIDTH].reshape(b, s, ATT_HEADS, ATT_HEAD_DIM)
    k = proj[..., ATT_Q_WIDTH:ATT_Q_WIDTH + ATT_KV_WIDTH].reshape(b, s, ATT_KV_HEADS, ATT_HEAD_DIM)
    v = proj[..., ATT_Q_WIDTH + ATT_KV_WIDTH:].reshape(b, s, ATT_KV_HEADS, ATT_HEAD_DIM)
    q = apply_axial_rope(rms_norm(q, q_gain), cos, sin) * (ATT_HEAD_DIM ** -0.5)
    k = apply_axial_rope(rms_norm(k, k_gain), cos, sin)
    nb = s // ATT_Q_BLOCK
    q_blocks = jnp.moveaxis(q.reshape(b, nb, ATT_Q_BLOCK, ATT_KV_HEADS, ATT_GROUP, ATT_HEAD_DIM), 1, 0)

    def block(qb):
        scores = jnp.einsum('bqkgd,bskd->bkgqs', qb, k).astype(jnp.float32)
        probs = jax.nn.softmax(scores, axis=-1).astype(v.dtype)
        return jnp.einsum('bkgqs,bskd->bqkgd', probs, v)

    o = lax.map(block, q_blocks)
    o = jnp.moveaxis(o, 0, 1).reshape(b, s, ATT_Q_WIDTH)
    return o @ w_out


def squared_relu_mlp(xn, w_up, w_down):
    hdn = jax.nn.relu(xn @ w_up)
    return (hdn * hdn) @ w_down


def trunk(x, p, norm_mix, norm_mlp, dn_w_in, dn_conv, dn_a_log, dn_dt_bias, dn_out_norm, dn_w_out,
          at_w_in, at_q_norm, at_k_norm, at_w_out, mlp_w_up, mlp_w_down, ple_norm, ple_w_gate, ple_w_proj):
    cos, sin = axial_rope_tables(x.shape[1])
    h = x
    for i in range(DEPTH):
        xn = rms_norm(h, norm_mix[i])
        j = i // 2
        if i % 2 == 0:
            h = h + deltanet_mixer(xn, dn_w_in[j], dn_conv[j], dn_a_log[j], dn_dt_bias[j], dn_out_norm[j], dn_w_out[j])
        else:
            h = h + attention_mixer(xn, at_w_in[j], at_q_norm[j], at_k_norm[j], at_w_out[j], cos, sin)
        h = h + squared_relu_mlp(rms_norm(h, norm_mlp[i]), mlp_w_up[i], mlp_w_down[i])
        gate = jax.nn.sigmoid(rms_norm(h, ple_norm[i]) @ ple_w_gate[i])
        h = h + gate * (p[i] @ ple_w_proj[i])
    return h


def setup_inputs(seed: int = 0) -> dict:
    key = jax.random.key(seed)
    ks = jax.random.split(key, 24)
    f32 = jnp.float32

    def normal(k, shape, scale):
        return jax.random.normal(k, shape, f32) * scale

    def gain(k, shape):
        return 1.0 + 0.02 * jax.random.normal(k, shape, f32)

    dt = jnp.exp(jax.random.uniform(ks[8], (N_DN_LAYERS, 2, DN_V_HEADS), f32, math.log(1e-3), math.log(1e-1)))
    dt_bias = dt + jnp.log(-jnp.expm1(-dt))
    a_log = jnp.log(jax.random.uniform(ks[7], (N_DN_LAYERS, 2, DN_V_HEADS), f32, 1.0, 16.0))
    return {
        'x_prompt': normal(ks[0], (BATCH, SEQ, D_MODEL), 1.0),
        'x_sample': normal(ks[1], (DEC_BATCH, DEC_SEQ, D_MODEL), 1.0),
        'p_prompt': normal(ks[2], (DEPTH, BATCH, SEQ, PLE_DIM), 1.0),
        'p_sample': normal(ks[3], (DEPTH, DEC_BATCH, DEC_SEQ, PLE_DIM), 1.0),
        'norm_mix': gain(ks[4], (DEPTH, D_MODEL)),
        'norm_mlp': gain(ks[5], (DEPTH, D_MODEL)),
        'dn_w_in': normal(ks[6], (N_DN_LAYERS, D_MODEL, DN_IN_WIDTH), D_MODEL ** -0.5),
        'dn_conv': normal(ks[9], (N_DN_LAYERS, DN_CONV_W, DN_QKV_WIDTH), DN_CONV_W ** -0.5),
        'dn_a_log': a_log,
        'dn_dt_bias': dt_bias,
        'dn_out_norm': gain(ks[10], (N_DN_LAYERS, DN_V_DIM)),
        'dn_w_out': normal(ks[11], (N_DN_LAYERS, DN_VALUE_WIDTH, D_MODEL), DN_VALUE_WIDTH ** -0.5),
        'at_w_in': normal(ks[12], (N_ATT_LAYERS, D_MODEL, ATT_IN_WIDTH), D_MODEL ** -0.5),
        'at_q_norm': gain(ks[13], (N_ATT_LAYERS, ATT_HEAD_DIM)),
        'at_k_norm': gain(ks[14], (N_ATT_LAYERS, ATT_HEAD_DIM)),
        'at_w_out': normal(ks[15], (N_ATT_LAYERS, ATT_Q_WIDTH, D_MODEL), ATT_Q_WIDTH ** -0.5),
        'mlp_w_up': normal(ks[16], (DEPTH, D_MODEL, D_FF), D_MODEL ** -0.5),
        'mlp_w_down': normal(ks[17], (DEPTH, D_FF, D_MODEL), D_FF ** -0.5),
        'ple_norm': gain(ks[18], (DEPTH, D_MODEL)),
        'ple_w_gate': normal(ks[19], (DEPTH, D_MODEL, D_MODEL), D_MODEL ** -0.5),
        'ple_w_proj': normal(ks[20], (DEPTH, PLE_DIM, D_MODEL), PLE_DIM ** -0.5),
    }


def reference(x_prompt, x_sample, p_prompt, p_sample, norm_mix, norm_mlp, dn_w_in, dn_conv, dn_a_log,
              dn_dt_bias, dn_out_norm, dn_w_out, at_w_in, at_q_norm, at_k_norm, at_w_out, mlp_w_up,
              mlp_w_down, ple_norm, ple_w_gate, ple_w_proj):
    y_prompt = trunk(x_prompt, p_prompt, norm_mix, norm_mlp, dn_w_in, dn_conv, dn_a_log, dn_dt_bias,
                     dn_out_norm, dn_w_out, at_w_in, at_q_norm, at_k_norm, at_w_out, mlp_w_up, mlp_w_down,
                     ple_norm, ple_w_gate, ple_w_proj)
    y_sample = trunk(x_sample, p_sample, norm_mix, norm_mlp, dn_w_in, dn_conv, dn_a_log, dn_dt_bias,
                     dn_out_norm, dn_w_out, at_w_in, at_q_norm, at_k_norm, at_w_out, mlp_w_up, mlp_w_down,
                     ple_norm, ple_w_gate, ple_w_proj)
    return (y_prompt, y_sample)
```

```python
import functools
import math

import jax
import jax.numpy as jnp
from jax import lax
from jax.experimental import pallas as pl
from jax.experimental.pallas import tpu as pltpu

F32 = jnp.float32
BF16 = jnp.bfloat16

NORM_EPS = 1e-6
GRID_W = 64
ROPE_THETA = 10000.0
HEAD_DIM = 128
DN_K_HEADS = 16
DN_V_HEADS = 32
DN_CONV_W = 5
DN_CHUNK = 64
ATT_HEADS = 16
ATT_KV_HEADS = 8

V7X_VMEM_BUDGET = 60000 * 1024
COMPILER_SCRATCH = 8 * 1024 * 1024


def _pick(n, pref):
    t = pref
    while t > 8 and n % t:
        t //= 2
    assert n % t == 0, (n, pref)
    return t


def _params(semantics, est_bytes):
    limit = int(min(V7X_VMEM_BUDGET, est_bytes + COMPILER_SCRATCH))
    return pltpu.CompilerParams(dimension_semantics=semantics, vmem_limit_bytes=limit)


def _rms_rows(x, gain):
    ms = jnp.mean(x * x, axis=-1, keepdims=True)
    return x * lax.rsqrt(ms + NORM_EPS) * gain


def _norm_matmul_kernel(x_ref, g_ref, w_ref, o_ref, xn_ref):
    @pl.when(pl.program_id(1) == 0)
    def _():
        xn_ref[...] = _rms_rows(x_ref[...], g_ref[...]).astype(BF16)

    o_ref[...] = jnp.dot(xn_ref[...], w_ref[...], preferred_element_type=F32).astype(o_ref.dtype)


def norm_matmul(x, gain, w, out_dtype, tm_pref=1024, tn_pref=512):
    t, d = x.shape
    n = w.shape[1]
    tm, tn = _pick(t, tm_pref), _pick(n, tn_pref)
    osz = jnp.dtype(out_dtype).itemsize
    est = 2 * tm * d * 4 + tm * d * 2 + 2 * d * tn * 2 + 2 * tm * tn * osz + tm * tn * 4
    return pl.pallas_call(
        _norm_matmul_kernel,
        name="norm_matmul",
        grid=(t // tm, n // tn),
        in_specs=[pl.BlockSpec((tm, d), lambda i, j: (i, 0)),
                  pl.BlockSpec((1, d), lambda i, j: (0, 0)),
                  pl.BlockSpec((d, tn), lambda i, j: (0, j))],
        out_specs=pl.BlockSpec((tm, tn), lambda i, j: (i, j)),
        out_shape=jax.ShapeDtypeStruct((t, n), out_dtype),
        scratch_shapes=[pltpu.VMEM((tm, d), BF16)],
        compiler_params=_params(("parallel", "arbitrary"), est),
    )(x, gain.reshape(1, d), w)


def _matmul_res_kernel(a_ref, w_ref, r_ref, o_ref):
    o_ref[...] = r_ref[...] + jnp.dot(a_ref[...], w_ref[...], preferred_element_type=F32)


def matmul_res(a, w, res, tm_pref=1024, tn_pref=512):
    t, k = a.shape
    n = w.shape[1]
    tm, tn = _pick(t, tm_pref), _pick(n, tn_pref)
    est = 2 * tm * k * 2 + 2 * k * tn * 2 + 4 * tm * tn * 4 + tm * tn * 4
    return pl.pallas_call(
        _matmul_res_kernel,
        name="matmul_res",
        grid=(t // tm, n // tn),
        in_specs=[pl.BlockSpec((tm, k), lambda i, j: (i, 0)),
                  pl.BlockSpec((k, tn), lambda i, j: (0, j)),
                  pl.BlockSpec((tm, tn), lambda i, j: (i, j))],
        out_specs=pl.BlockSpec((tm, tn), lambda i, j: (i, j)),
        out_shape=jax.ShapeDtypeStruct((t, n), F32),
        compiler_params=_params(("parallel", "arbitrary"), est),
    )(a, w, res)


def _mlp_kernel(x_ref, g_ref, wu_ref, wd_ref, o_ref, xn_ref):
    @pl.when(pl.program_id(1) == 0)
    def _():
        x = x_ref[...]
        xn_ref[...] = _rms_rows(x, g_ref[...]).astype(BF16)
        o_ref[...] = x

    hid = jnp.maximum(jnp.dot(xn_ref[...], wu_ref[...], preferred_element_type=F32), 0.0)
    hid = (hid * hid).astype(BF16)
    o_ref[...] += jnp.dot(hid, wd_ref[...], preferred_element_type=F32)


def mlp(x, gain, w_up, w_down, tm_pref=512, tf_pref=512):
    t, d = x.shape
    f = w_up.shape[1]
    tm, tf = _pick(t, tm_pref), _pick(f, tf_pref)
    est = 4 * tm * d * 4 + tm * d * 2 + 4 * d * tf * 2 + tm * tf * 6 + tm * d * 4
    return pl.pallas_call(
        _mlp_kernel,
        name="mlp",
        grid=(t // tm, f // tf),
        in_specs=[pl.BlockSpec((tm, d), lambda i, j: (i, 0)),
                  pl.BlockSpec((1, d), lambda i, j: (0, 0)),
                  pl.BlockSpec((d, tf), lambda i, j: (0, j)),
                  pl.BlockSpec((tf, d), lambda i, j: (j, 0))],
        out_specs=pl.BlockSpec((tm, d), lambda i, j: (i, 0)),
        out_shape=jax.ShapeDtypeStruct((t, d), F32),
        scratch_shapes=[pltpu.VMEM((tm, d), BF16)],
        compiler_params=_params(("parallel", "arbitrary"), est),
    )(x, gain.reshape(1, d), w_up, w_down)


def _ple_kernel(x_ref, g_ref, wg_ref, p_ref, wp_ref, r_ref, o_ref, xn_ref):
    @pl.when(pl.program_id(1) == 0)
    def _():
        xn_ref[...] = _rms_rows(x_ref[...], g_ref[...]).astype(BF16)

    gate = jax.nn.sigmoid(jnp.dot(xn_ref[...], wg_ref[...], preferred_element_type=F32))
    emb = jnp.dot(p_ref[...].astype(BF16), wp_ref[...], preferred_element_type=F32)
    o_ref[...] = r_ref[...] + gate * emb


def ple(x, gain, w_gate, p, w_proj, tm_pref=1024, tn_pref=512):
    t, d = x.shape
    pd = p.shape[1]
    tm, tn = _pick(t, tm_pref), _pick(d, tn_pref)
    est = (2 * tm * d * 4 + tm * d * 2 + 2 * d * tn * 2 + 2 * tm * pd * 4 + 2 * pd * tn * 2
           + 4 * tm * tn * 4 + 2 * tm * tn * 4)
    return pl.pallas_call(
        _ple_kernel,
        name="ple",
        grid=(t // tm, d // tn),
        in_specs=[pl.BlockSpec((tm, d), lambda i, j: (i, 0)),
                  pl.BlockSpec((1, d), lambda i, j: (0, 0)),
                  pl.BlockSpec((d, tn), lambda i, j: (0, j)),
                  pl.BlockSpec((tm, pd), lambda i, j: (i, 0)),
                  pl.BlockSpec((pd, tn), lambda i, j: (0, j)),
                  pl.BlockSpec((tm, tn), lambda i, j: (i, j))],
        out_specs=pl.BlockSpec((tm, tn), lambda i, j: (i, j)),
        out_shape=jax.ShapeDtypeStruct((t, d), F32),
        scratch_shapes=[pltpu.VMEM((tm, d), BF16)],
        compiler_params=_params(("parallel", "arbitrary"), est),
    )(x, gain.reshape(1, d), w_gate, p, w_proj, x)


def _qk_rope_kernel(x_ref, g_ref, c_ref, s_ref, o_ref):
    y = _rms_rows(x_ref[...].astype(F32), g_ref[0])
    o_ref[...] = (y * c_ref[...] + pltpu.roll(y, HEAD_DIM // 2, axis=1) * s_ref[...]).astype(o_ref.dtype)


def qk_norm_rope(qkv, gains, cos2, sin2, seq):
    t = qkv.shape[0]
    nh = gains.shape[0]
    tm = _pick(seq, 1024)
    per_seq = seq // tm
    est = 2 * tm * HEAD_DIM * (2 + 4 + 4 + 2) + 6 * tm * HEAD_DIM * 4
    return pl.pallas_call(
        _qk_rope_kernel,
        name="qk_rope",
        grid=(t // tm, nh),
        in_specs=[pl.BlockSpec((tm, HEAD_DIM), lambda i, j: (i, j)),
                  pl.BlockSpec((1, 1, HEAD_DIM), lambda i, j: (j, 0, 0)),
                  pl.BlockSpec((tm, HEAD_DIM), lambda i, j: (i % per_seq, 0)),
                  pl.BlockSpec((tm, HEAD_DIM), lambda i, j: (i % per_seq, 0))],
        out_specs=pl.BlockSpec((tm, HEAD_DIM), lambda i, j: (i, j)),
        out_shape=jax.ShapeDtypeStruct((t, nh * HEAD_DIM), BF16),
        compiler_params=_params(("parallel", "arbitrary"), est),
    )(qkv, gains.reshape(nh, 1, HEAD_DIM), cos2, sin2)


def _attn_kernel(q_ref, k_ref, v_ref, o_ref, *, tk):
    tq = q_ref.shape[1]
    seq = k_ref.shape[1]
    q = q_ref[0]
    q2 = jnp.concatenate([q[:, :HEAD_DIM], q[:, HEAD_DIM:]], axis=0)

    def body(c, carry):
        m, l, acc = carry
        start = pl.multiple_of(c * tk, tk)
        kc = k_ref[0, pl.ds(start, tk), :]
        vc = v_ref[0, pl.ds(start, tk), :]
        s = lax.dot_general(q2, kc, (((1,), (1,)), ((), ())), preferred_element_type=F32)
        m_new = jnp.maximum(m, jnp.max(s, axis=-1, keepdims=True))
        alpha = jnp.exp(m - m_new)
        p = jnp.exp(s - m_new)
        l = alpha * l + jnp.sum(p, axis=-1, keepdims=True)
        acc = alpha * acc + jnp.dot(p.astype(BF16), vc, preferred_element_type=F32)
        return m_new, l, acc

    m0 = jnp.full((2 * tq, 1), -jnp.inf, F32)
    l0 = jnp.zeros((2 * tq, 1), F32)
    a0 = jnp.zeros((2 * tq, HEAD_DIM), F32)
    _, l, acc = lax.fori_loop(0, seq // tk, body, (m0, l0, a0))
    o = acc / l
    o_ref[0] = jnp.concatenate([o[:tq], o[tq:]], axis=1).astype(o_ref.dtype)


def attention(qk, qkv, batch, seq, tq_pref=512, tk_pref=512):
    qk3 = qk.reshape(batch, seq, qk.shape[1])
    qkv3 = qkv.reshape(batch, seq, qkv.shape[1])
    group = ATT_HEADS // ATT_KV_HEADS
    qw = group * HEAD_DIM
    tq, tk = _pick(seq, tq_pref), _pick(seq, tk_pref)
    k_blk0 = ATT_HEADS
    v_blk0 = ATT_HEADS + ATT_KV_HEADS
    est = (4 * tq * qw * 2 + 4 * seq * HEAD_DIM * 2 + 3 * group * tq * tk * 4
           + 4 * group * tq * HEAD_DIM * 4)
    out = pl.pallas_call(
        functools.partial(_attn_kernel, tk=tk),
        name="attention",
        grid=(batch, ATT_KV_HEADS, seq // tq),
        in_specs=[pl.BlockSpec((1, tq, qw), lambda b, h, i: (b, i, h)),
                  pl.BlockSpec((1, seq, HEAD_DIM), lambda b, h, i: (b, 0, k_blk0 + h)),
                  pl.BlockSpec((1, seq, HEAD_DIM), lambda b, h, i: (b, 0, v_blk0 + h))],
        out_specs=pl.BlockSpec((1, tq, qw), lambda b, h, i: (b, i, h)),
        out_shape=jax.ShapeDtypeStruct((batch, seq, ATT_HEADS * HEAD_DIM), BF16),
        compiler_params=_params(("parallel", "parallel", "arbitrary"), est),
    )(qk3, qk3, qkv3)
    return out.reshape(batch * seq, ATT_HEADS * HEAD_DIM)


def _dn_conv_kernel(x_ref, w_ref, o_ref, xs_ref, *, rows, n_norm_blocks, q_blocks):
    seq = x_ref.shape[1]
    pad = 8
    win = rows + 2 * pad
    zeros = jnp.zeros((pad, HEAD_DIM), F32)
    xs_ref[0:pad, :] = zeros
    xs_ref[seq + pad:seq + 2 * pad, :] = zeros
    xs_ref[pad:seq + pad, :] = x_ref[0].astype(F32)
    w = w_ref[...]
    j = pl.program_id(1)
    q_scale = jnp.where(j < q_blocks, HEAD_DIM ** -0.5, 1.0).astype(F32)

    def conv_silu(r):
        base = pl.multiple_of(r * rows, rows)
        xw = xs_ref[pl.ds(base, win), :]
        acc = jnp.zeros((rows, HEAD_DIM), F32)
        for tap in range(DN_CONV_W):
            shifted = pltpu.roll(xw, win - (pad - DN_CONV_W // 2 + tap), axis=0)[0:rows]
            acc = acc + w[tap:tap + 1, :] * shifted
        return base, jax.nn.silu(acc)

    @pl.when(j < n_norm_blocks)
    def _():
        def body(r, carry):
            base, y = conv_silu(r)
            ss = jnp.sum(y * y, axis=-1, keepdims=True)
            o_ref[0, pl.ds(base, rows), :] = (y * lax.rsqrt(ss + NORM_EPS) * q_scale).astype(o_ref.dtype)
            return carry
        lax.fori_loop(0, seq // rows, body, 0)

    @pl.when(j >= n_norm_blocks)
    def _():
        def body(r, carry):
            base, y = conv_silu(r)
            o_ref[0, pl.ds(base, rows), :] = y.astype(o_ref.dtype)
            return carry
        lax.fori_loop(0, seq // rows, body, 0)


def dn_conv(proj3, conv_w):
    batch, seq, _ = proj3.shape
    width = conv_w.shape[1]
    nblk = width // HEAD_DIM
    rows = _pick(seq, 256)
    w8 = jnp.zeros((8, width), F32).at[:DN_CONV_W].set(conv_w.astype(F32))
    est = 4 * seq * HEAD_DIM * 2 + (seq + 16) * HEAD_DIM * 4 + 16 * rows * HEAD_DIM * 4
    return pl.pallas_call(
        functools.partial(_dn_conv_kernel, rows=rows, n_norm_blocks=2 * DN_K_HEADS, q_blocks=DN_K_HEADS),
        name="dn_conv",
        grid=(batch, nblk),
        in_specs=[pl.BlockSpec((1, seq, HEAD_DIM), lambda b, j: (b, 0, j)),
                  pl.BlockSpec((8, HEAD_DIM), lambda b, j: (0, j))],
        out_specs=pl.BlockSpec((1, seq, HEAD_DIM), lambda b, j: (b, 0, j)),
        out_shape=jax.ShapeDtypeStruct((batch, seq, width), BF16),
        scratch_shapes=[pltpu.VMEM((seq + 16, HEAD_DIM), F32)],
        compiler_params=_params(("parallel", "arbitrary"), est),
    )(proj3, w8)


def _dn_gates_kernel(ba_ref, alog_ref, dtb_ref, col_ref, row_ref):
    ba = ba_ref[0]
    tg = ba.shape[0]
    lane = lax.broadcasted_iota(jnp.int32, ba.shape, 1)
    pos = lax.broadcasted_iota(jnp.int32, ba.shape, 0) & (DN_CHUNK - 1)
    beta = jax.nn.sigmoid(ba)
    g = -jnp.exp(alog_ref[...]) * jax.nn.softplus(ba + dtb_ref[...])
    g = jnp.where(lane >= 2 * DN_V_HEADS, g, 0.0)
    pre = g
    suf = g
    k = 1
    while k < DN_CHUNK:
        pre = pre + jnp.where(pos >= k, pltpu.roll(pre, k, axis=0), 0.0)
        suf = suf + jnp.where(pos < DN_CHUNK - k, pltpu.roll(suf, tg - k, axis=0), 0.0)
        k *= 2
    out = jnp.where(lane < 2 * DN_V_HEADS, beta, jnp.where(lane < 3 * DN_V_HEADS, pre, suf))
    col_ref[0] = out
    row_ref[0] = out.T


def dn_gates(ba3, a_log, dt_bias):
    batch, seq, w = ba3.shape
    tg = _pick(seq, 256)
    zeros = jnp.zeros((2 * DN_V_HEADS,), F32)
    alog_row = jnp.concatenate([zeros, a_log.astype(F32).reshape(-1)]).reshape(1, w)
    dtb_row = jnp.concatenate([zeros, dt_bias.astype(F32).reshape(-1)]).reshape(1, w)
    est = 24 * tg * w * 4
    return pl.pallas_call(
        _dn_gates_kernel,
        name="dn_gates",
        grid=(batch, seq // tg),
        in_specs=[pl.BlockSpec((1, tg, w), lambda b, i: (b, i, 0)),
                  pl.BlockSpec((1, w), lambda b, i: (0, 0)),
                  pl.BlockSpec((1, w), lambda b, i: (0, 0))],
        out_specs=[pl.BlockSpec((1, tg, w), lambda b, i: (b, i, 0)),
                   pl.BlockSpec((1, w, tg), lambda b, i: (b, 0, i))],
        out_shape=[jax.ShapeDtypeStruct((batch, seq, w), F32),
                   jax.ShapeDtypeStruct((batch, w, seq), F32)],
        compiler_params=_params(("parallel", "arbitrary"), est),
    )(ba3, alog_row, dtb_row)


def _tri_masks(backward):
    c = DN_CHUNK
    i = lax.broadcasted_iota(jnp.int32, (c, c), 0)
    j = lax.broadcasted_iota(jnp.int32, (c, c), 1)
    if backward:
        i, j = j, i
    incl = i >= j
    strict = i > j
    levels = []
    s = 1
    while s < c:
        same = (i // (2 * s)) == (j // (2 * s))
        levels.append(same & ((i % (2 * s)) >= s) & ((j % (2 * s)) < s))
        s *= 2
    return incl, strict, levels


def _unit_tri_inverse(lmat, levels):
    c = DN_CHUNK
    i = lax.broadcasted_iota(jnp.int32, (c, c), 0)
    j = lax.broadcasted_iota(jnp.int32, (c, c), 1)
    t = jnp.where(i == j, 1.0, 0.0).astype(F32) - jnp.where(levels[0], lmat, 0.0)
    for mask in levels[1:]:
        tb = t.astype(BF16)
        off = jnp.where(mask, lmat, 0.0).astype(BF16)
        x = jnp.dot(tb, off, preferred_element_type=F32).astype(BF16)
        t = t - jnp.dot(x, tb, preferred_element_type=F32)
    return t


def _dn_direction(backward, hk, q_ref, k_ref, v_ref, col_ref, row_ref, o_ref, st_ref):
    c = DN_CHUNK
    n_chunks = q_ref.shape[1] // c
    incl, strict, levels = _tri_masks(backward)
    cols = col_ref[0]
    lane = lax.broadcasted_iota(jnp.int32, cols.shape, 1)
    dir_off = DN_V_HEADS if backward else 0
    heads = []
    for e in range(2):
        head = 2 * hk + e
        beta_col = jnp.sum(jnp.where(lane == dir_off + head, cols, 0.0), axis=1, keepdims=True)
        g_lane = 2 * DN_V_HEADS + dir_off + head
        gc_col = jnp.sum(jnp.where(lane == g_lane, cols, 0.0), axis=1, keepdims=True)
        gc_row = row_ref[0, pl.ds(g_lane, 1), :]
        heads.append((beta_col, gc_col, gc_row))

    prepped = []
    for ci in range(n_chunks):
        r0 = ci * c
        kc = k_ref[0, r0:r0 + c, :]
        qc = q_ref[0, r0:r0 + c, :]
        kf = kc.astype(F32)
        qf = qc.astype(F32)
        kk = lax.dot_general(kc, kc, (((1,), (1,)), ((), ())), preferred_element_type=F32)
        qk = lax.dot_general(qc, kc, (((1,), (1,)), ((), ())), preferred_element_type=F32)
        per_head = []
        for e in range(2):
            beta_col, gc_col, gc_row = heads[e]
            bcol = beta_col[r0:r0 + c]
            gcol = gc_col[r0:r0 + c]
            grow = gc_row[:, r0:r0 + c]
            g_last = gcol[0:1] if backward else gcol[c - 1:c]
            decay = jnp.exp(jnp.where(incl, gcol - grow, -jnp.inf))
            lmat = jnp.where(strict, bcol * kk * decay, 0.0)
            tinv = _unit_tri_inverse(lmat, levels)
            vf = v_ref[0, r0:r0 + c, e * HEAD_DIM:(e + 1) * HEAD_DIM].astype(F32)
            rhs = jnp.concatenate([vf * bcol, kf * (bcol * jnp.exp(gcol))], axis=1).astype(BF16)
            sol = jnp.dot(tinv.astype(BF16), rhs, preferred_element_type=F32)
            u = sol[:, :HEAD_DIM]
            w = sol[:, HEAD_DIM:].astype(BF16)
            intra = jnp.where(incl, qk * decay, 0.0).astype(BF16)
            q_dec = (qf * jnp.exp(gcol)).astype(BF16)
            k_dec_t = (kf * jnp.exp(g_last - gcol)).T.astype(BF16)
            per_head.append((u, w, intra, q_dec, k_dec_t, jnp.exp(g_last)))
        prepped.append(per_head)

    order = range(n_chunks - 1, -1, -1) if backward else range(n_chunks)
    for e in range(2):
        slot = (2 if backward else 0) + e
        state = st_ref[slot]
        for ci in order:
            r0 = ci * c
            u, w, intra, q_dec, k_dec_t, e_last = prepped[ci][e]
            sb = state.astype(BF16)
            v_new = u - jnp.dot(w, sb, preferred_element_type=F32)
            vb = v_new.astype(BF16)
            o = (jnp.dot(q_dec, sb, preferred_element_type=F32)
                 + jnp.dot(intra, vb, preferred_element_type=F32))
            o_ref[0, r0:r0 + c, e * HEAD_DIM:(e + 1) * HEAD_DIM] = o.astype(o_ref.dtype)
            state = state * e_last + jnp.dot(k_dec_t, vb, preferred_element_type=F32)
        st_ref[slot] = state


def _dn_core_kernel(qf_ref, kf_ref, vf_ref, colf_ref, rowf_ref,
                    qb_ref, kb_ref, vb_ref, colb_ref, rowb_ref,
                    of_ref, ob_ref, st_ref):
    @pl.when(pl.program_id(2) == 0)
    def _():
        st_ref[...] = jnp.zeros(st_ref.shape, F32)

    hk = pl.program_id(1)
    _dn_direction(False, hk, qf_ref, kf_ref, vf_ref, colf_ref, rowf_ref, of_ref, st_ref)
    _dn_direction(True, hk, qb_ref, kb_ref, vb_ref, colb_ref, rowb_ref, ob_ref, st_ref)


def dn_core(qkv_c, gcol, grow, gt_pref=256):
    batch, seq, _ = qkv_c.shape
    gt = _pick(seq, gt_pref)
    ng = seq // gt
    vw = 2 * HEAD_DIM
    k_blk0 = DN_K_HEADS
    v_blk0 = DN_K_HEADS
    gw = gcol.shape[2]

    def fwd(spec_cols):
        return lambda b, h, c: (b, c, spec_cols(h))

    def bwd(spec_cols):
        return lambda b, h, c: (b, ng - 1 - c, spec_cols(h))

    def specs(order):
        return [pl.BlockSpec((1, gt, HEAD_DIM), order(lambda h: h)),
                pl.BlockSpec((1, gt, HEAD_DIM), order(lambda h: k_blk0 + h)),
                pl.BlockSpec((1, gt, vw), order(lambda h: v_blk0 + h)),
                pl.BlockSpec((1, gt, gw), order(lambda h: 0)),
                pl.BlockSpec((1, gw, gt), (lambda b, h, c: (b, 0, c)) if order is fwd
                             else (lambda b, h, c: (b, 0, ng - 1 - c)))]

    out_sd = jax.ShapeDtypeStruct((batch, seq, DN_V_HEADS * HEAD_DIM), BF16)
    est = 4 * gt * (2 * HEAD_DIM * 2 + vw * 2 + 2 * gw * 4) + 4 * gt * vw * 2 + 64 * gt * HEAD_DIM * 4
    return pl.pallas_call(
        _dn_core_kernel,
        name="dn_core",
        grid=(batch, DN_K_HEADS, ng),
        in_specs=specs(fwd) + specs(bwd),
        out_specs=[pl.BlockSpec((1, gt, vw), lambda b, h, c: (b, c, h)),
                   pl.BlockSpec((1, gt, vw), lambda b, h, c: (b, ng - 1 - c, h))],
        out_shape=[out_sd, out_sd],
        scratch_shapes=[pltpu.VMEM((4, HEAD_DIM, HEAD_DIM), F32)],
        compiler_params=_params(("parallel", "parallel", "arbitrary"), est),
    )(qkv_c, qkv_c, qkv_c, gcol, grow, qkv_c, qkv_c, qkv_c, gcol, grow)


def _dn_out_kernel(of_ref, ob_ref, z_ref, g_ref, w_ref, r_ref, o_ref, a_ref):
    @pl.when(pl.program_id(1) == 0)
    def _():
        gain = g_ref[...]
        for h in range(a_ref.shape[1] // HEAD_DIM):
            sl = slice(h * HEAD_DIM, (h + 1) * HEAD_DIM)
            o = of_ref[:, sl].astype(F32) + ob_ref[:, sl].astype(F32)
            y = _rms_rows(o, gain) * jax.nn.silu(z_ref[:, sl].astype(F32))
            a_ref[:, sl] = y.astype(BF16)

    o_ref[...] = r_ref[...] + jnp.dot(a_ref[...], w_ref[...], preferred_element_type=F32)


def dn_out(o_f, o_b, proj, z_col0, out_gain, w_out, res, tm_pref=512, tn_pref=512):
    t, kdim = o_f.shape
    n = w_out.shape[1]
    tm, tn = _pick(t, tm_pref), _pick(n, tn_pref)
    z_blk = z_col0 // kdim
    assert z_col0 % kdim == 0
    est = 6 * tm * kdim * 2 + tm * kdim * 2 + 2 * kdim * tn * 2 + 4 * tm * tn * 4 + 8 * tm * HEAD_DIM * 4
    return pl.pallas_call(
        _dn_out_kernel,
        name="dn_out",
        grid=(t // tm, n // tn),
        in_specs=[pl.BlockSpec((tm, kdim), lambda i, j: (i, 0)),
                  pl.BlockSpec((tm, kdim), lambda i, j: (i, 0)),
                  pl.BlockSpec((tm, kdim), lambda i, j: (i, z_blk)),
                  pl.BlockSpec((1, HEAD_DIM), lambda i, j: (0, 0)),
                  pl.BlockSpec((kdim, tn), lambda i, j: (0, j)),
                  pl.BlockSpec((tm, tn), lambda i, j: (i, j))],
        out_specs=pl.BlockSpec((tm, tn), lambda i, j: (i, j)),
        out_shape=jax.ShapeDtypeStruct((t, n), F32),
        scratch_shapes=[pltpu.VMEM((tm, kdim), BF16)],
        compiler_params=_params(("parallel", "arbitrary"), est),
    )(o_f, o_b, proj, out_gain.reshape(1, HEAD_DIM), w_out, res)


def _deltanet_layer(h, batch, seq, norm_gain, w_in_main, w_in_ba, conv_w, a_log, dt_bias, out_gain, w_out):
    proj = norm_matmul(h, norm_gain, w_in_main, BF16)
    ba = norm_matmul(h, norm_gain, w_in_ba, F32, tn_pref=128)
    qkv_c = dn_conv(proj.reshape(batch, seq, -1), conv_w)
    gcol, grow = dn_gates(ba.reshape(batch, seq, -1), a_log, dt_bias)
    o_f, o_b = dn_core(qkv_c, gcol, grow)
    vwidth = DN_V_HEADS * HEAD_DIM
    return dn_out(o_f.reshape(batch * seq, vwidth), o_b.reshape(batch * seq, vwidth), proj,
                  conv_w.shape[1], out_gain, w_out, h)


def _attention_layer(h, batch, seq, norm_gain, w_in, qk_gains, w_out, cos2, sin2):
    qkv = norm_matmul(h, norm_gain, w_in, BF16)
    qk = qk_norm_rope(qkv, qk_gains, cos2, sin2, seq)
    o = attention(qk, qkv, batch, seq)
    return matmul_res(o, w_out, h)


def _rope_tables(n):
    rows = n // GRID_W
    row = jnp.repeat(jnp.arange(rows, dtype=F32), GRID_W)
    col = jnp.tile(jnp.arange(GRID_W, dtype=F32), rows)
    half = HEAD_DIM // 2
    inv_freq = ROPE_THETA ** (-jnp.arange(0, half, 2, dtype=F32) / half)
    ang = jnp.concatenate([row[:, None] * inv_freq, col[:, None] * inv_freq], axis=-1)
    cos, sin = jnp.cos(ang), jnp.sin(ang)
    return jnp.concatenate([cos, cos], axis=-1), jnp.concatenate([-sin, sin], axis=-1)


def _deinterleave_heads(n_heads):
    within = jnp.concatenate([jnp.arange(0, HEAD_DIM, 2), jnp.arange(1, HEAD_DIM, 2)])
    return (jnp.arange(n_heads)[:, None] * HEAD_DIM + within[None, :]).reshape(-1)


def _prepare_weights(dn_w_in, dn_w_out, at_w_in, at_q_norm, at_k_norm, at_w_out, mlp_w_up, mlp_w_down,
                     ple_w_gate, ple_w_proj):
    qkvz = dn_w_in.shape[2] - 4 * DN_V_HEADS
    n_qk = ATT_HEADS + ATT_KV_HEADS
    perm = jnp.concatenate([_deinterleave_heads(n_qk),
                            jnp.arange(n_qk * HEAD_DIM, at_w_in.shape[2])])
    within = _deinterleave_heads(1)
    scale = HEAD_DIM ** -0.5
    qk_gains = jnp.concatenate(
        [jnp.broadcast_to((at_q_norm.astype(F32) * scale)[:, None, within], (at_q_norm.shape[0], ATT_HEADS, HEAD_DIM)),
         jnp.broadcast_to(at_k_norm.astype(F32)[:, None, within], (at_k_norm.shape[0], ATT_KV_HEADS, HEAD_DIM))],
        axis=1)
    return dict(
        dn_w_in_main=dn_w_in[:, :, :qkvz].astype(BF16),
        dn_w_in_ba=dn_w_in[:, :, qkvz:].astype(BF16),
        dn_w_out=dn_w_out.astype(BF16),
        at_w_in=at_w_in[:, :, perm].astype(BF16),
        qk_gains=qk_gains,
        at_w_out=at_w_out.astype(BF16),
        mlp_w_up=mlp_w_up.astype(BF16),
        mlp_w_down=mlp_w_down.astype(BF16),
        ple_w_gate=ple_w_gate.astype(BF16),
        ple_w_proj=ple_w_proj.astype(BF16),
    )


def _trunk(x, p, wts, norm_mix, norm_mlp, dn_conv_w, dn_a_log, dn_dt_bias, dn_out_norm, ple_norm):
    batch, seq, d = x.shape
    depth = p.shape[0]
    cos2, sin2 = _rope_tables(seq)
    h = x.reshape(batch * seq, d)
    for i in range(depth):
        j = i // 2
        if i % 2 == 0:
            h = _deltanet_layer(h, batch, seq, norm_mix[i], wts['dn_w_in_main'][j], wts['dn_w_in_ba'][j],
                                dn_conv_w[j], dn_a_log[j], dn_dt_bias[j], dn_out_norm[j], wts['dn_w_out'][j])
        else:
            h = _attention_layer(h, batch, seq, norm_mix[i], wts['at_w_in'][j], wts['qk_gains'][j],
                                 wts['at_w_out'][j], cos2, sin2)
        h = mlp(h, norm_mlp[i], wts['mlp_w_up'][i], wts['mlp_w_down'][i])
        h = ple(h, ple_norm[i], wts['ple_w_gate'][i], p[i].reshape(batch * seq, -1), wts['ple_w_proj'][i])
    return h.reshape(batch, seq, d)


def kernel(x_prompt, x_sample, p_prompt, p_sample, norm_mix, norm_mlp, dn_w_in, dn_conv, dn_a_log, dn_dt_bias, dn_out_norm, dn_w_out, at_w_in, at_q_norm, at_k_norm, at_w_out, mlp_w_up, mlp_w_down, ple_norm, ple_w_gate, ple_w_proj):
    wts = _prepare_weights(dn_w_in, dn_w_out, at_w_in, at_q_norm, at_k_norm, at_w_out, mlp_w_up, mlp_w_down,
                           ple_w_gate, ple_w_proj)
    shared = (wts, norm_mix, norm_mlp, dn_conv, dn_a_log, dn_dt_bias, dn_out_norm, ple_norm)
    return (_trunk(x_prompt, p_prompt, *shared), _trunk(x_sample, p_sample, *shared))
```

```python
import functools
import math

import jax
import jax.numpy as jnp
from jax import lax
from jax.experimental import pallas as pl
from jax.experimental.pallas import tpu as pltpu

F32 = jnp.float32
BF16 = jnp.bfloat16

NORM_EPS = 1e-6
GRID_W = 64
ROPE_THETA = 10000.0
HEAD_DIM = 128
DN_K_HEADS = 16
DN_V_HEADS = 32
DN_CONV_W = 5
DN_CHUNK = 64
ATT_HEADS = 16
ATT_KV_HEADS = 8

V7X_VMEM_BUDGET = 60000 * 1024
COMPILER_SCRATCH = 8 * 1024 * 1024


def _pick(n, pref):
    t = pref
    while t > 8 and n % t:
        t //= 2
    assert n % t == 0, (n, pref)
    return t


def _params(semantics, est_bytes):
    limit = int(min(V7X_VMEM_BUDGET, est_bytes + COMPILER_SCRATCH))
    return pltpu.CompilerParams(dimension_semantics=semantics, vmem_limit_bytes=limit)


def _rms_rows(x, gain):
    ms = jnp.mean(x * x, axis=-1, keepdims=True)
    return x * lax.rsqrt(ms + NORM_EPS) * gain


def _norm_matmul_kernel(x_ref, g_ref, w_ref, o_ref, xn_ref):
    @pl.when(pl.program_id(1) == 0)
    def _():
        xn_ref[...] = _rms_rows(x_ref[...], g_ref[...]).astype(BF16)

    o_ref[...] = jnp.dot(xn_ref[...], w_ref[...], preferred_element_type=F32).astype(o_ref.dtype)


def norm_matmul(x, gain, w, out_dtype, tm_pref=1024, tn_pref=512):
    t, d = x.shape
    n = w.shape[1]
    tm, tn = _pick(t, tm_pref), _pick(n, tn_pref)
    osz = jnp.dtype(out_dtype).itemsize
    est = 2 * tm * d * 4 + tm * d * 2 + 2 * d * tn * 2 + 2 * tm * tn * osz + tm * tn * 4
    return pl.pallas_call(
        _norm_matmul_kernel,
        name="norm_matmul",
        grid=(t // tm, n // tn),
        in_specs=[pl.BlockSpec((tm, d), lambda i, j: (i, 0)),
                  pl.BlockSpec((1, d), lambda i, j: (0, 0)),
                  pl.BlockSpec((d, tn), lambda i, j: (0, j))],
        out_specs=pl.BlockSpec((tm, tn), lambda i, j: (i, j)),
        out_shape=jax.ShapeDtypeStruct((t, n), out_dtype),
        scratch_shapes=[pltpu.VMEM((tm, d), BF16)],
        compiler_params=_params(("parallel", "arbitrary"), est),
    )(x, gain.reshape(1, d), w)


def _matmul_res_kernel(a_ref, w_ref, r_ref, o_ref):
    o_ref[...] = r_ref[...] + jnp.dot(a_ref[...], w_ref[...], preferred_element_type=F32)


def matmul_res(a, w, res, tm_pref=1024, tn_pref=512):
    t, k = a.shape
    n = w.shape[1]
    tm, tn = _pick(t, tm_pref), _pick(n, tn_pref)
    est = 2 * tm * k * 2 + 2 * k * tn * 2 + 4 * tm * tn * 4 + tm * tn * 4
    return pl.pallas_call(
        _matmul_res_kernel,
        name="matmul_res",
        grid=(t // tm, n // tn),
        in_specs=[pl.BlockSpec((tm, k), lambda i, j: (i, 0)),
                  pl.BlockSpec((k, tn), lambda i, j: (0, j)),
                  pl.BlockSpec((tm, tn), lambda i, j: (i, j))],
        out_specs=pl.BlockSpec((tm, tn), lambda i, j: (i, j)),
        out_shape=jax.ShapeDtypeStruct((t, n), F32),
        compiler_params=_params(("parallel", "arbitrary"), est),
    )(a, w, res)


def _mlp_kernel(x_ref, g_ref, wu_ref, wd_ref, o_ref, xn_ref):
    @pl.when(pl.program_id(1) == 0)
    def _():
        x = x_ref[...]
        xn_ref[...] = _rms_rows(x, g_ref[...]).astype(BF16)
        o_ref[...] = x

    hid = jnp.maximum(jnp.dot(xn_ref[...], wu_ref[...], preferred_element_type=F32), 0.0)
    hid = (hid * hid).astype(BF16)
    o_ref[...] += jnp.dot(hid, wd_ref[...], preferred_element_type=F32)


def mlp(x, gain, w_up, w_down, tm_pref=512, tf_pref=512):
    t, d = x.shape
    f = w_up.shape[1]
    tm, tf = _pick(t, tm_pref), _pick(f, tf_pref)
    est = 4 * tm * d * 4 + tm * d * 2 + 4 * d * tf * 2 + tm * tf * 6 + tm * d * 4
    return pl.pallas_call(
        _mlp_kernel,
        name="mlp",
        grid=(t // tm, f // tf),
        in_specs=[pl.BlockSpec((tm, d), lambda i, j: (i, 0)),
                  pl.BlockSpec((1, d), lambda i, j: (0, 0)),
                  pl.BlockSpec((d, tf), lambda i, j: (0, j)),
                  pl.BlockSpec((tf, d), lambda i, j: (j, 0))],
        out_specs=pl.BlockSpec((tm, d), lambda i, j: (i, 0)),
        out_shape=jax.ShapeDtypeStruct((t, d), F32),
        scratch_shapes=[pltpu.VMEM((tm, d), BF16)],
        compiler_params=_params(("parallel", "arbitrary"), est),
    )(x, gain.reshape(1, d), w_up, w_down)


def _ple_kernel(x_ref, g_ref, wg_ref, p_ref, wp_ref, r_ref, o_ref, xn_ref):
    @pl.when(pl.program_id(1) == 0)
    def _():
        xn_ref[...] = _rms_rows(x_ref[...], g_ref[...]).astype(BF16)

    gate = jax.nn.sigmoid(jnp.dot(xn_ref[...], wg_ref[...], preferred_element_type=F32))
    emb = jnp.dot(p_ref[...].astype(BF16), wp_ref[...], preferred_element_type=F32)
    o_ref[...] = r_ref[...] + gate * emb


def ple(x, gain, w_gate, p, w_proj, tm_pref=1024, tn_pref=512):
    t, d = x.shape
    pd = p.shape[1]
    tm, tn = _pick(t, tm_pref), _pick(d, tn_pref)
    est = (2 * tm * d * 4 + tm * d * 2 + 2 * d * tn * 2 + 2 * tm * pd * 4 + 2 * pd * tn * 2
           + 4 * tm * tn * 4 + 2 * tm * tn * 4)
    return pl.pallas_call(
        _ple_kernel,
        name="ple",
        grid=(t // tm, d // tn),
        in_specs=[pl.BlockSpec((tm, d), lambda i, j: (i, 0)),
                  pl.BlockSpec((1, d), lambda i, j: (0, 0)),
                  pl.BlockSpec((d, tn), lambda i, j: (0, j)),
                  pl.BlockSpec((tm, pd), lambda i, j: (i, 0)),
                  pl.BlockSpec((pd, tn), lambda i, j: (0, j)),
                  pl.BlockSpec((tm, tn), lambda i, j: (i, j))],
        out_specs=pl.BlockSpec((tm, tn), lambda i, j: (i, j)),
        out_shape=jax.ShapeDtypeStruct((t, d), F32),
        scratch_shapes=[pltpu.VMEM((tm, d), BF16)],
        compiler_params=_params(("parallel", "arbitrary"), est),
    )(x, gain.reshape(1, d), w_gate, p, w_proj, x)


def _qk_rope_kernel(x_ref, g_ref, c_ref, s_ref, o_ref):
    y = _rms_rows(x_ref[...].astype(F32), g_ref[0])
    o_ref[...] = (y * c_ref[...] + pltpu.roll(y, HEAD_DIM // 2, axis=1) * s_ref[...]).astype(o_ref.dtype)


def qk_norm_rope(qkv, gains, cos2, sin2, seq):
    t = qkv.shape[0]
    nh = gains.shape[0]
    tm = _pick(seq, 1024)
    per_seq = seq // tm
    est = 2 * tm * HEAD_DIM * (2 + 4 + 4 + 2) + 6 * tm * HEAD_DIM * 4
    return pl.pallas_call(
        _qk_rope_kernel,
        name="qk_rope",
        grid=(t // tm, nh),
        in_specs=[pl.BlockSpec((tm, HEAD_DIM), lambda i, j: (i, j)),
                  pl.BlockSpec((1, 1, HEAD_DIM), lambda i, j: (j, 0, 0)),
                  pl.BlockSpec((tm, HEAD_DIM), lambda i, j: (i % per_seq, 0)),
                  pl.BlockSpec((tm, HEAD_DIM), lambda i, j: (i % per_seq, 0))],
        out_specs=pl.BlockSpec((tm, HEAD_DIM), lambda i, j: (i, j)),
        out_shape=jax.ShapeDtypeStruct((t, nh * HEAD_DIM), BF16),
        compiler_params=_params(("parallel", "arbitrary"), est),
    )(qkv, gains.reshape(nh, 1, HEAD_DIM), cos2, sin2)


def _attn_kernel(q_ref, k_ref, v_ref, o_ref, vt_ref, qt_ref, acc_ref, m_ref, l_ref, s_ref, *, tk, cb):
    tq = q_ref.shape[1]
    seq = k_ref.shape[1]
    r = qt_ref.shape[1]

    @pl.when(pl.program_id(2) == 0)
    def _():
        for c in range(seq // tk):
            vt_ref[c] = v_ref[0, c * tk:(c + 1) * tk, :].astype(F32).T.astype(BF16)

    q = q_ref[0]
    q2 = jnp.concatenate([q[:, :HEAD_DIM], q[:, HEAD_DIM:]], axis=0)
    qt_ref[...] = q2.astype(F32).T.astype(BF16)
    m_ref[...] = jnp.full(m_ref.shape, -jnp.inf, F32)
    l_ref[...] = jnp.zeros(l_ref.shape, F32)
    acc_ref[...] = jnp.zeros(acc_ref.shape, F32)

    blocks = [slice(b * cb, (b + 1) * cb) for b in range(r // cb)]
    n_chunks = seq // tk

    def scores(c, slot):
        start = c * tk if isinstance(c, int) else pl.multiple_of(c * tk, tk)
        kc = k_ref[0, pl.ds(start, tk), :]
        for sl in blocks:
            s_ref[slot, :, sl] = jnp.dot(kc, qt_ref[:, sl], preferred_element_type=F32)

    def softmax_pv(c, slot):
        vtc = vt_ref[c]
        s = [s_ref[slot, :, sl] for sl in blocks]
        m_old = [m_ref[:, sl] for sl in blocks]
        m_new = [jnp.maximum(mo, jnp.max(sb, axis=0, keepdims=True)) for mo, sb in zip(m_old, s)]
        alpha = [jnp.exp2(mo - mn) for mo, mn in zip(m_old, m_new)]
        p = [jnp.exp2(sb - mn) for sb, mn in zip(s, m_new)]
        for sl, a, pb, mn in zip(blocks, alpha, p, m_new):
            l_ref[:, sl] = a * l_ref[:, sl] + jnp.sum(pb, axis=0, keepdims=True)
            m_ref[:, sl] = mn
        return alpha, [jnp.dot(vtc, pb.astype(BF16), preferred_element_type=F32) for pb in p]

    def accumulate(alpha, pv):
        for sl, a, pvb in zip(blocks, alpha, pv):
            acc_ref[:, sl] = a * acc_ref[:, sl] + pvb

    def pair(c2, cur, with_next):
        c = 2 * c2
        nxt = 2 - cur
        part_a = softmax_pv(c, cur)
        if with_next:
            scores(c + 2, nxt)
        part_b = softmax_pv(c + 1, cur + 1)
        if with_next:
            scores(c + 3, nxt + 1)
        accumulate(*part_a)
        accumulate(*part_b)

    n_quads = n_chunks // 4
    scores(0, 0)
    scores(1, 1)

    def body(i, carry):
        pair(2 * i, 0, True)
        pair(2 * i + 1, 2, True)
        return carry

    lax.fori_loop(0, n_quads - 1, body, 0)
    pair(2 * n_quads - 2, 0, True)
    pair(2 * n_quads - 1, 2, False)
    o = (acc_ref[...] / l_ref[...]).T
    o_ref[0] = jnp.concatenate([o[:tq], o[tq:]], axis=1).astype(o_ref.dtype)


def attention(qk, qkv, batch, seq, tq_pref=512, tk_pref=256, cb_pref=256):
    qk3 = qk.reshape(batch, seq, qk.shape[1])
    qkv3 = qkv.reshape(batch, seq, qkv.shape[1])
    group = ATT_HEADS // ATT_KV_HEADS
    qw = group * HEAD_DIM
    tq, tk = _pick(seq, tq_pref), _pick(seq // 4, tk_pref)
    assert (seq // tk) % 4 == 0
    r = group * tq
    cb = _pick(r, cb_pref)
    k_blk0 = ATT_HEADS
    v_blk0 = ATT_HEADS + ATT_KV_HEADS
    est = (4 * tq * qw * 2 + 4 * seq * HEAD_DIM * 2 + seq * HEAD_DIM * 2 + r * HEAD_DIM * 6
           + 8 * tk * cb * 4 + 4 * r * HEAD_DIM * 4 + 4 * tk * r * 4)
    out = pl.pallas_call(
        functools.partial(_attn_kernel, tk=tk, cb=cb),
        name="attention",
        grid=(batch, ATT_KV_HEADS, seq // tq),
        in_specs=[pl.BlockSpec((1, tq, qw), lambda b, h, i: (b, i, h)),
                  pl.BlockSpec((1, seq, HEAD_DIM), lambda b, h, i: (b, 0, k_blk0 + h)),
                  pl.BlockSpec((1, seq, HEAD_DIM), lambda b, h, i: (b, 0, v_blk0 + h))],
        out_specs=pl.BlockSpec((1, tq, qw), lambda b, h, i: (b, i, h)),
        out_shape=jax.ShapeDtypeStruct((batch, seq, ATT_HEADS * HEAD_DIM), BF16),
        scratch_shapes=[pltpu.VMEM((seq // tk, HEAD_DIM, tk), BF16),
                        pltpu.VMEM((HEAD_DIM, r), BF16),
                        pltpu.VMEM((HEAD_DIM, r), F32),
                        pltpu.VMEM((1, r), F32),
                        pltpu.VMEM((1, r), F32),
                        pltpu.VMEM((4, tk, r), F32)],
        compiler_params=_params(("parallel", "parallel", "arbitrary"), est),
    )(qk3, qk3, qkv3)
    return out.reshape(batch * seq, ATT_HEADS * HEAD_DIM)


def _dn_conv_kernel(x_ref, w_ref, o_ref, xs_ref, *, rows, n_norm_blocks, q_blocks):
    seq = x_ref.shape[1]
    pad = 8
    win = rows + 2 * pad
    zeros = jnp.zeros((pad, HEAD_DIM), F32)
    xs_ref[0:pad, :] = zeros
    xs_ref[seq + pad:seq + 2 * pad, :] = zeros
    xs_ref[pad:seq + pad, :] = x_ref[0].astype(F32)
    w = w_ref[...]
    j = pl.program_id(1)
    q_scale = jnp.where(j < q_blocks, HEAD_DIM ** -0.5, 1.0).astype(F32)

    def conv_silu(r):
        base = pl.multiple_of(r * rows, rows)
        xw = xs_ref[pl.ds(base, win), :]
        acc = jnp.zeros((rows, HEAD_DIM), F32)
        for tap in range(DN_CONV_W):
            shifted = pltpu.roll(xw, win - (pad - DN_CONV_W // 2 + tap), axis=0)[0:rows]
            acc = acc + w[tap:tap + 1, :] * shifted
        return base, jax.nn.silu(acc)

    @pl.when(j < n_norm_blocks)
    def _():
        def body(r, carry):
            base, y = conv_silu(r)
            ss = jnp.sum(y * y, axis=-1, keepdims=True)
            o_ref[0, pl.ds(base, rows), :] = (y * lax.rsqrt(ss + NORM_EPS) * q_scale).astype(o_ref.dtype)
            return carry
        lax.fori_loop(0, seq // rows, body, 0)

    @pl.when(j >= n_norm_blocks)
    def _():
        def body(r, carry):
            base, y = conv_silu(r)
            o_ref[0, pl.ds(base, rows), :] = y.astype(o_ref.dtype)
            return carry
        lax.fori_loop(0, seq // rows, body, 0)


def dn_conv(proj3, conv_w):
    batch, seq, _ = proj3.shape
    width = conv_w.shape[1]
    nblk = width // HEAD_DIM
    rows = _pick(seq, 256)
    w8 = jnp.zeros((8, width), F32).at[:DN_CONV_W].set(conv_w.astype(F32))
    est = 4 * seq * HEAD_DIM * 2 + (seq + 16) * HEAD_DIM * 4 + 16 * rows * HEAD_DIM * 4
    return pl.pallas_call(
        functools.partial(_dn_conv_kernel, rows=rows, n_norm_blocks=2 * DN_K_HEADS, q_blocks=DN_K_HEADS),
        name="dn_conv",
        grid=(batch, nblk),
        in_specs=[pl.BlockSpec((1, seq, HEAD_DIM), lambda b, j: (b, 0, j)),
                  pl.BlockSpec((8, HEAD_DIM), lambda b, j: (0, j))],
        out_specs=pl.BlockSpec((1, seq, HEAD_DIM), lambda b, j: (b, 0, j)),
        out_shape=jax.ShapeDtypeStruct((batch, seq, width), BF16),
        scratch_shapes=[pltpu.VMEM((seq + 16, HEAD_DIM), F32)],
        compiler_params=_params(("parallel", "arbitrary"), est),
    )(proj3, w8)


def _dn_gates_kernel(ba_ref, alog_ref, dtb_ref, col_ref, row_ref):
    ba = ba_ref[0]
    tg = ba.shape[0]
    lane = lax.broadcasted_iota(jnp.int32, ba.shape, 1)
    pos = lax.broadcasted_iota(jnp.int32, ba.shape, 0) & (DN_CHUNK - 1)
    beta = jax.nn.sigmoid(ba)
    g = -jnp.exp(alog_ref[...]) * jax.nn.softplus(ba + dtb_ref[...])
    g = jnp.where(lane >= 2 * DN_V_HEADS, g, 0.0)
    pre = g
    suf = g
    k = 1
    while k < DN_CHUNK:
        pre = pre + jnp.where(pos >= k, pltpu.roll(pre, k, axis=0), 0.0)
        suf = suf + jnp.where(pos < DN_CHUNK - k, pltpu.roll(suf, tg - k, axis=0), 0.0)
        k *= 2
    out = jnp.where(lane < 2 * DN_V_HEADS, beta, jnp.where(lane < 3 * DN_V_HEADS, pre, suf))
    col_ref[0] = out
    row_ref[0] = out.T


def dn_gates(ba3, a_log, dt_bias):
    batch, seq, w = ba3.shape
    tg = _pick(seq, 256)
    zeros = jnp.zeros((2 * DN_V_HEADS,), F32)
    alog_row = jnp.concatenate([zeros, a_log.astype(F32).reshape(-1)]).reshape(1, w)
    dtb_row = jnp.concatenate([zeros, dt_bias.astype(F32).reshape(-1)]).reshape(1, w)
    est = 24 * tg * w * 4
    return pl.pallas_call(
        _dn_gates_kernel,
        name="dn_gates",
        grid=(batch, seq // tg),
        in_specs=[pl.BlockSpec((1, tg, w), lambda b, i: (b, i, 0)),
                  pl.BlockSpec((1, w), lambda b, i: (0, 0)),
                  pl.BlockSpec((1, w), lambda b, i: (0, 0))],
        out_specs=[pl.BlockSpec((1, tg, w), lambda b, i: (b, i, 0)),
                   pl.BlockSpec((1, w, tg), lambda b, i: (b, 0, i))],
        out_shape=[jax.ShapeDtypeStruct((batch, seq, w), F32),
                   jax.ShapeDtypeStruct((batch, w, seq), F32)],
        compiler_params=_params(("parallel", "arbitrary"), est),
    )(ba3, alog_row, dtb_row)


def _pair_masks(backward):
    c = DN_CHUNK
    i = lax.broadcasted_iota(jnp.int32, (c, 2 * c), 0)
    lane = lax.broadcasted_iota(jnp.int32, (c, 2 * c), 1)
    j = lane & (c - 1)
    lo = lane < c
    if backward:
        i, j = j, i
    levels = []
    s = 1
    while s < c:
        sh = int(math.log2(2 * s))
        same = lax.shift_right_logical(i, sh) == lax.shift_right_logical(j, sh)
        levels.append(same & ((i & (2 * s - 1)) >= s) & ((j & (2 * s - 1)) < s))
        s *= 2
    return lo, i == j, i >= j, i > j, levels


def _block_diag2(x, lo):
    return jnp.concatenate([jnp.where(lo, x, 0.0), jnp.where(lo, 0.0, x)], axis=0).astype(BF16)


def _block_diag_wide(x):
    half = x.shape[1] // 2
    z = jnp.zeros((x.shape[0], half), x.dtype)
    return jnp.concatenate([jnp.concatenate([x[:, :half], z], axis=1),
                            jnp.concatenate([z, x[:, half:]], axis=1)], axis=0)


def _dn_chunk_setup(backward, hk, q_ref, k_ref, v_ref, col_ref, row_ref, o_ref):
    c = DN_CHUNK
    nt = (((1,), (1,)), ((), ()))
    n_chunks = q_ref.shape[1] // c
    lo, eye, incl, strict, levels = _pair_masks(backward)
    lo_row = lo[0:1]
    cols = col_ref[0]
    lane = lax.broadcasted_iota(jnp.int32, cols.shape, 1)
    dir_off = DN_V_HEADS if backward else 0
    beta_lane = dir_off + 2 * hk
    g_lane = 2 * DN_V_HEADS + dir_off + 2 * hk

    def column(ln):
        return jnp.sum(jnp.where(lane == ln, cols, 0.0), axis=1, keepdims=True)

    beta_cols = (column(beta_lane), column(beta_lane + 1))
    gc_cols = (column(g_lane), column(g_lane + 1))
    gc_rows = (row_ref[0, pl.ds(g_lane, 1), :], row_ref[0, pl.ds(g_lane + 1, 1), :])
    zero = jnp.zeros((c, HEAD_DIM), F32)
    chunks = []
    for ci in range(n_chunks):
        r0 = ci * c
        v0 = (r0 // (2 * c)) * 2 * c
        row_a = gc_rows[0][:, v0:v0 + 2 * c]
        row_b = gc_rows[1][:, v0:v0 + 2 * c]
        if r0 == v0:
            grow = jnp.where(lo_row, row_a, pltpu.roll(row_b, c, axis=1))
        else:
            grow = jnp.where(lo_row, pltpu.roll(row_a, c, axis=1), row_b)
        bcol = [x[r0:r0 + c] for x in beta_cols]
        gcol = [x[r0:r0 + c] for x in gc_cols]
        g_last = [g[0:1] if backward else g[c - 1:c] for g in gcol]
        decay = jnp.exp(jnp.where(incl, jnp.where(lo, gcol[0], gcol[1]) - grow, -jnp.inf))
        kc = k_ref[0, r0:r0 + c, :]
        qc = q_ref[0, r0:r0 + c, :]
        kf = kc.astype(F32)
        qf = qc.astype(F32)
        vp = v_ref[0, r0:r0 + c, :].astype(F32)
        kb = [kf * (bcol[e] * jnp.exp(gcol[e])) for e in range(2)]
        vb = [vp[:, e * HEAD_DIM:(e + 1) * HEAD_DIM] * bcol[e] for e in range(2)]
        rhs = jnp.concatenate([jnp.concatenate([vb[0], zero, kb[0], zero], axis=1),
                               jnp.concatenate([zero, vb[1], zero, kb[1]], axis=1)], axis=0).astype(BF16)
        q_dec = jnp.concatenate([qf * jnp.exp(gcol[0]), qf * jnp.exp(gcol[1])], axis=1).astype(BF16)
        k_dec = jnp.concatenate([kf * jnp.exp(g_last[e] - gcol[e]) for e in range(2)], axis=0)
        e_last = jnp.concatenate([jnp.broadcast_to(jnp.exp(g_last[e]), (1, HEAD_DIM)) for e in range(2)], axis=1)
        kc2 = jnp.concatenate([kc, kc], axis=0)
        chunks.append(dict(
            r0=r0, o_ref=o_ref, lo=lo, eye=eye, incl=incl, strict=strict, levels=levels,
            beta=jnp.where(lo, bcol[0], bcol[1]), decay=decay, rhs=rhs, q_dec=q_dec,
            k_dec_t=k_dec.T.astype(BF16), e_last=e_last,
            kk=lax.dot_general(kc, kc2, nt, preferred_element_type=F32),
            qk=lax.dot_general(qc, kc2, nt, preferred_element_type=F32)))
    return chunks


def _dn_core_kernel(qf_ref, kf_ref, vf_ref, colf_ref, rowf_ref,
                    qb_ref, kb_ref, vb_ref, colb_ref, rowb_ref,
                    of_ref, ob_ref, st_ref):
    @pl.when(pl.program_id(2) == 0)
    def _():
        st_ref[...] = jnp.zeros(st_ref.shape, F32)

    c = DN_CHUNK
    hk = pl.program_id(1)
    per_dir = [_dn_chunk_setup(False, hk, qf_ref, kf_ref, vf_ref, colf_ref, rowf_ref, of_ref),
               _dn_chunk_setup(True, hk, qb_ref, kb_ref, vb_ref, colb_ref, rowb_ref, ob_ref)]
    insts = per_dir[0] + per_dir[1]
    n_levels = len(insts[0]['levels'])

    for it in insts:
        lmat = jnp.where(it['strict'], it['beta'] * it['kk'] * it['decay'], 0.0)
        it['lmat'] = lmat
        it['t'] = jnp.where(it['eye'], 1.0, 0.0).astype(F32) - jnp.where(it['levels'][0], lmat, 0.0)
        it['intra'] = jnp.where(it['incl'], it['qk'] * it['decay'], 0.0).astype(BF16)
    for lv in range(1, n_levels):
        for it in insts:
            off = _block_diag2(jnp.where(it['levels'][lv], it['lmat'], 0.0), it['lo'])
            it['x'] = jnp.dot(it['t'].astype(BF16), off, preferred_element_type=F32).astype(BF16)
        for it in insts:
            it['t'] = it['t'] - jnp.dot(it['x'], _block_diag2(it['t'], it['lo']), preferred_element_type=F32)
    for it in insts:
        sol = jnp.dot(it['t'].astype(BF16), it['rhs'], preferred_element_type=F32)
        it['u'] = sol[:, :2 * HEAD_DIM]
        it['lhs_state'] = jnp.concatenate([sol[:, 2 * HEAD_DIM:].astype(BF16), it['q_dec']], axis=0)
        it['lhs_vnew'] = jnp.concatenate([it['intra'], it['k_dec_t']], axis=0)

    n_chunks = len(per_dir[0])
    states = [st_ref[0], st_ref[1]]
    for s in range(n_chunks):
        step = [per_dir[0][s], per_dir[1][n_chunks - 1 - s]]
        from_state = [jnp.dot(it['lhs_state'], _block_diag_wide(states[d].astype(BF16)),
                              preferred_element_type=F32) for d, it in enumerate(step)]
        v_new = [(it['u'] - from_state[d][:c]).astype(BF16) for d, it in enumerate(step)]
        from_vnew = [jnp.dot(it['lhs_vnew'], _block_diag_wide(v_new[d]), preferred_element_type=F32)
                     for d, it in enumerate(step)]
        for d, it in enumerate(step):
            o = from_state[d][c:] + from_vnew[d][:c]
            it['o_ref'][0, it['r0']:it['r0'] + c, :] = o.astype(it['o_ref'].dtype)
            states[d] = states[d] * it['e_last'] + from_vnew[d][c:]
    st_ref[0] = states[0]
    st_ref[1] = states[1]


def dn_core(qkv_c, gcol, grow, gt_pref=512):
    batch, seq, _ = qkv_c.shape
    gt = _pick(seq, gt_pref)
    ng = seq // gt
    vw = 2 * HEAD_DIM
    k_blk0 = DN_K_HEADS
    v_blk0 = DN_K_HEADS
    gw = gcol.shape[2]

    def fwd(spec_cols):
        return lambda b, h, c: (b, c, spec_cols(h))

    def bwd(spec_cols):
        return lambda b, h, c: (b, ng - 1 - c, spec_cols(h))

    def specs(order):
        return [pl.BlockSpec((1, gt, HEAD_DIM), order(lambda h: h)),
                pl.BlockSpec((1, gt, HEAD_DIM), order(lambda h: k_blk0 + h)),
                pl.BlockSpec((1, gt, vw), order(lambda h: v_blk0 + h)),
                pl.BlockSpec((1, gt, gw), order(lambda h: 0)),
                pl.BlockSpec((1, gw, gt), (lambda b, h, c: (b, 0, c)) if order is fwd
                             else (lambda b, h, c: (b, 0, ng - 1 - c)))]

    out_sd = jax.ShapeDtypeStruct((batch, seq, DN_V_HEADS * HEAD_DIM), BF16)
    est = 4 * gt * (2 * HEAD_DIM * 2 + vw * 2 + 2 * gw * 4) + 4 * gt * vw * 2 + 64 * gt * HEAD_DIM * 4
    return pl.pallas_call(
        _dn_core_kernel,
        name="dn_core",
        grid=(batch, DN_K_HEADS, ng),
        in_specs=specs(fwd) + specs(bwd),
        out_specs=[pl.BlockSpec((1, gt, vw), lambda b, h, c: (b, c, h)),
                   pl.BlockSpec((1, gt, vw), lambda b, h, c: (b, ng - 1 - c, h))],
        out_shape=[out_sd, out_sd],
        scratch_shapes=[pltpu.VMEM((2, HEAD_DIM, 2 * HEAD_DIM), F32)],
        compiler_params=_params(("parallel", "parallel", "arbitrary"), est),
    )(qkv_c, qkv_c, qkv_c, gcol, grow, qkv_c, qkv_c, qkv_c, gcol, grow)


def _dn_out_kernel(of_ref, ob_ref, z_ref, g_ref, w_ref, r_ref, o_ref, a_ref):
    @pl.when(pl.program_id(1) == 0)
    def _():
        gain = g_ref[...]
        for h in range(a_ref.shape[1] // HEAD_DIM):
            sl = slice(h * HEAD_DIM, (h + 1) * HEAD_DIM)
            o = of_ref[:, sl].astype(F32) + ob_ref[:, sl].astype(F32)
            y = _rms_rows(o, gain) * jax.nn.silu(z_ref[:, sl].astype(F32))
            a_ref[:, sl] = y.astype(BF16)

    o_ref[...] = r_ref[...] + jnp.dot(a_ref[...], w_ref[...], preferred_element_type=F32)


def dn_out(o_f, o_b, proj, z_col0, out_gain, w_out, res, tm_pref=512, tn_pref=512):
    t, kdim = o_f.shape
    n = w_out.shape[1]
    tm, tn = _pick(t, tm_pref), _pick(n, tn_pref)
    z_blk = z_col0 // kdim
    assert z_col0 % kdim == 0
    est = 6 * tm * kdim * 2 + tm * kdim * 2 + 2 * kdim * tn * 2 + 4 * tm * tn * 4 + 8 * tm * HEAD_DIM * 4
    return pl.pallas_call(
        _dn_out_kernel,
        name="dn_out",
        grid=(t // tm, n // tn),
        in_specs=[pl.BlockSpec((tm, kdim), lambda i, j: (i, 0)),
                  pl.BlockSpec((tm, kdim), lambda i, j: (i, 0)),
                  pl.BlockSpec((tm, kdim), lambda i, j: (i, z_blk)),
                  pl.BlockSpec((1, HEAD_DIM), lambda i, j: (0, 0)),
                  pl.BlockSpec((kdim, tn), lambda i, j: (0, j)),
                  pl.BlockSpec((tm, tn), lambda i, j: (i, j))],
        out_specs=pl.BlockSpec((tm, tn), lambda i, j: (i, j)),
        out_shape=jax.ShapeDtypeStruct((t, n), F32),
        scratch_shapes=[pltpu.VMEM((tm, kdim), BF16)],
        compiler_params=_params(("parallel", "arbitrary"), est),
    )(o_f, o_b, proj, out_gain.reshape(1, HEAD_DIM), w_out, res)


def _deltanet_layer(h, batch, seq, norm_gain, w_in_main, w_in_ba, conv_w, a_log, dt_bias, out_gain, w_out):
    proj = norm_matmul(h, norm_gain, w_in_main, BF16)
    ba = norm_matmul(h, norm_gain, w_in_ba, F32, tn_pref=128)
    qkv_c = dn_conv(proj.reshape(batch, seq, -1), conv_w)
    gcol, grow = dn_gates(ba.reshape(batch, seq, -1), a_log, dt_bias)
    o_f, o_b = dn_core(qkv_c, gcol, grow)
    vwidth = DN_V_HEADS * HEAD_DIM
    return dn_out(o_f.reshape(batch * seq, vwidth), o_b.reshape(batch * seq, vwidth), proj,
                  conv_w.shape[1], out_gain, w_out, h)


def _attention_layer(h, batch, seq, norm_gain, w_in, qk_gains, w_out, cos2, sin2):
    qkv = norm_matmul(h, norm_gain, w_in, BF16)
    qk = qk_norm_rope(qkv, qk_gains, cos2, sin2, seq)
    o = attention(qk, qkv, batch, seq)
    return matmul_res(o, w_out, h)


def _rope_tables(n):
    rows = n // GRID_W
    row = jnp.repeat(jnp.arange(rows, dtype=F32), GRID_W)
    col = jnp.tile(jnp.arange(GRID_W, dtype=F32), rows)
    half = HEAD_DIM // 2
    inv_freq = ROPE_THETA ** (-jnp.arange(0, half, 2, dtype=F32) / half)
    ang = jnp.concatenate([row[:, None] * inv_freq, col[:, None] * inv_freq], axis=-1)
    cos, sin = jnp.cos(ang), jnp.sin(ang)
    return jnp.concatenate([cos, cos], axis=-1), jnp.concatenate([-sin, sin], axis=-1)


def _deinterleave_heads(n_heads):
    within = jnp.concatenate([jnp.arange(0, HEAD_DIM, 2), jnp.arange(1, HEAD_DIM, 2)])
    return (jnp.arange(n_heads)[:, None] * HEAD_DIM + within[None, :]).reshape(-1)


def _prepare_weights(dn_w_in, dn_w_out, at_w_in, at_q_norm, at_k_norm, at_w_out, mlp_w_up, mlp_w_down,
                     ple_w_gate, ple_w_proj):
    qkvz = dn_w_in.shape[2] - 4 * DN_V_HEADS
    n_qk = ATT_HEADS + ATT_KV_HEADS
    perm = jnp.concatenate([_deinterleave_heads(n_qk),
                            jnp.arange(n_qk * HEAD_DIM, at_w_in.shape[2])])
    within = _deinterleave_heads(1)
    scale = HEAD_DIM ** -0.5 * math.log2(math.e)
    qk_gains = jnp.concatenate(
        [jnp.broadcast_to((at_q_norm.astype(F32) * scale)[:, None, within], (at_q_norm.shape[0], ATT_HEADS, HEAD_DIM)),
         jnp.broadcast_to(at_k_norm.astype(F32)[:, None, within], (at_k_norm.shape[0], ATT_KV_HEADS, HEAD_DIM))],
        axis=1)
    return dict(
        dn_w_in_main=dn_w_in[:, :, :qkvz].astype(BF16),
        dn_w_in_ba=dn_w_in[:, :, qkvz:].astype(BF16),
        dn_w_out=dn_w_out.astype(BF16),
        at_w_in=at_w_in[:, :, perm].astype(BF16),
        qk_gains=qk_gains,
        at_w_out=at_w_out.astype(BF16),
        mlp_w_up=mlp_w_up.astype(BF16),
        mlp_w_down=mlp_w_down.astype(BF16),
        ple_w_gate=ple_w_gate.astype(BF16),
        ple_w_proj=ple_w_proj.astype(BF16),
    )


def _trunk(x, p, wts, norm_mix, norm_mlp, dn_conv_w, dn_a_log, dn_dt_bias, dn_out_norm, ple_norm):
    batch, seq, d = x.shape
    depth = p.shape[0]
    cos2, sin2 = _rope_tables(seq)
    h = x.reshape(batch * seq, d)
    for i in range(depth):
        j = i // 2
        if i % 2 == 0:
            h = _deltanet_layer(h, batch, seq, norm_mix[i], wts['dn_w_in_main'][j], wts['dn_w_in_ba'][j],
                                dn_conv_w[j], dn_a_log[j], dn_dt_bias[j], dn_out_norm[j], wts['dn_w_out'][j])
        else:
            h = _attention_layer(h, batch, seq, norm_mix[i], wts['at_w_in'][j], wts['qk_gains'][j],
                                 wts['at_w_out'][j], cos2, sin2)
        h = mlp(h, norm_mlp[i], wts['mlp_w_up'][i], wts['mlp_w_down'][i])
        h = ple(h, ple_norm[i], wts['ple_w_gate'][i], p[i].reshape(batch * seq, -1), wts['ple_w_proj'][i])
    return h.reshape(batch, seq, d)


def kernel(x_prompt, x_sample, p_prompt, p_sample, norm_mix, norm_mlp, dn_w_in, dn_conv, dn_a_log, dn_dt_bias, dn_out_norm, dn_w_out, at_w_in, at_q_norm, at_k_norm, at_w_out, mlp_w_up, mlp_w_down, ple_norm, ple_w_gate, ple_w_proj):
    wts = _prepare_weights(dn_w_in, dn_w_out, at_w_in, at_q_norm, at_k_norm, at_w_out, mlp_w_up, mlp_w_down,
                           ple_w_gate, ple_w_proj)
    shared = (wts, norm_mix, norm_mlp, dn_conv, dn_a_log, dn_dt_bias, dn_out_norm, ple_norm)
    return (_trunk(x_prompt, p_prompt, *shared), _trunk(x_sample, p_sample, *shared))
```

```python
import functools
import math

import jax
import jax.numpy as jnp
from jax import lax
from jax.experimental import pallas as pl
from jax.experimental.pallas import tpu as pltpu

F32 = jnp.float32
BF16 = jnp.bfloat16

NORM_EPS = 1e-6
GRID_W = 64
ROPE_THETA = 10000.0
HEAD_DIM = 128
DN_K_HEADS = 16
DN_V_HEADS = 32
DN_CONV_W = 5
DN_CHUNK = 64
ATT_HEADS = 16
ATT_KV_HEADS = 8

V7X_VMEM_BUDGET = 60000 * 1024
COMPILER_SCRATCH = 8 * 1024 * 1024


def _pick(n, pref):
    t = pref
    while t > 8 and n % t:
        t //= 2
    assert n % t == 0, (n, pref)
    return t


def _params(semantics, est_bytes):
    limit = int(min(V7X_VMEM_BUDGET, est_bytes + COMPILER_SCRATCH))
    return pltpu.CompilerParams(dimension_semantics=semantics, vmem_limit_bytes=limit)


def _rms_rows(x, gain):
    ms = jnp.mean(x * x, axis=-1, keepdims=True)
    return x * lax.rsqrt(ms + NORM_EPS) * gain


def _norm_matmul_kernel(x_ref, g_ref, w_ref, o_ref, xn_ref):
    @pl.when(pl.program_id(1) == 0)
    def _():
        xn_ref[...] = _rms_rows(x_ref[...], g_ref[...]).astype(BF16)

    o_ref[...] = jnp.dot(xn_ref[...], w_ref[...], preferred_element_type=F32).astype(o_ref.dtype)


def norm_matmul(x, gain, w, out_dtype, tm_pref=1024, tn_pref=512):
    t, d = x.shape
    n = w.shape[1]
    tm, tn = _pick(t, tm_pref), _pick(n, tn_pref)
    osz = jnp.dtype(out_dtype).itemsize
    est = 2 * tm * d * 4 + tm * d * 2 + 2 * d * tn * 2 + 2 * tm * tn * osz + tm * tn * 4
    return pl.pallas_call(
        _norm_matmul_kernel,
        name="norm_matmul",
        grid=(t // tm, n // tn),
        in_specs=[pl.BlockSpec((tm, d), lambda i, j: (i, 0)),
                  pl.BlockSpec((1, d), lambda i, j: (0, 0)),
                  pl.BlockSpec((d, tn), lambda i, j: (0, j))],
        out_specs=pl.BlockSpec((tm, tn), lambda i, j: (i, j)),
        out_shape=jax.ShapeDtypeStruct((t, n), out_dtype),
        scratch_shapes=[pltpu.VMEM((tm, d), BF16)],
        compiler_params=_params(("parallel", "arbitrary"), est),
    )(x, gain.reshape(1, d), w)


def _matmul_res_kernel(a_ref, w_ref, r_ref, o_ref):
    o_ref[...] = r_ref[...] + jnp.dot(a_ref[...], w_ref[...], preferred_element_type=F32)


def matmul_res(a, w, res, tm_pref=1024, tn_pref=512):
    t, k = a.shape
    n = w.shape[1]
    tm, tn = _pick(t, tm_pref), _pick(n, tn_pref)
    est = 2 * tm * k * 2 + 2 * k * tn * 2 + 4 * tm * tn * 4 + tm * tn * 4
    return pl.pallas_call(
        _matmul_res_kernel,
        name="matmul_res",
        grid=(t // tm, n // tn),
        in_specs=[pl.BlockSpec((tm, k), lambda i, j: (i, 0)),
                  pl.BlockSpec((k, tn), lambda i, j: (0, j)),
                  pl.BlockSpec((tm, tn), lambda i, j: (i, j))],
        out_specs=pl.BlockSpec((tm, tn), lambda i, j: (i, j)),
        out_shape=jax.ShapeDtypeStruct((t, n), F32),
        compiler_params=_params(("parallel", "arbitrary"), est),
    )(a, w, res)


def _mlp_kernel(x_ref, g_ref, wu_ref, wd_ref, o_ref, xn_ref):
    @pl.when(pl.program_id(1) == 0)
    def _():
        x = x_ref[...]
        xn_ref[...] = _rms_rows(x, g_ref[...]).astype(BF16)
        o_ref[...] = x

    hid = jnp.maximum(jnp.dot(xn_ref[...], wu_ref[...], preferred_element_type=F32), 0.0)
    hid = (hid * hid).astype(BF16)
    o_ref[...] += jnp.dot(hid, wd_ref[...], preferred_element_type=F32)


def mlp(x, gain, w_up, w_down, tm_pref=512, tf_pref=512):
    t, d = x.shape
    f = w_up.shape[1]
    tm, tf = _pick(t, tm_pref), _pick(f, tf_pref)
    est = 4 * tm * d * 4 + tm * d * 2 + 4 * d * tf * 2 + tm * tf * 6 + tm * d * 4
    return pl.pallas_call(
        _mlp_kernel,
        name="mlp",
        grid=(t // tm, f // tf),
        in_specs=[pl.BlockSpec((tm, d), lambda i, j: (i, 0)),
                  pl.BlockSpec((1, d), lambda i, j: (0, 0)),
                  pl.BlockSpec((d, tf), lambda i, j: (0, j)),
                  pl.BlockSpec((tf, d), lambda i, j: (j, 0))],
        out_specs=pl.BlockSpec((tm, d), lambda i, j: (i, 0)),
        out_shape=jax.ShapeDtypeStruct((t, d), F32),
        scratch_shapes=[pltpu.VMEM((tm, d), BF16)],
        compiler_params=_params(("parallel", "arbitrary"), est),
    )(x, gain.reshape(1, d), w_up, w_down)


def _ple_kernel(x_ref, g_ref, wg_ref, p_ref, wp_ref, o_ref, *, sub):
    for s0 in range(0, x_ref.shape[0], sub):
        rows = slice(s0, s0 + sub)
        x = x_ref[rows, :]
        xn = _rms_rows(x, g_ref[...]).astype(BF16)
        gate = jax.nn.sigmoid(jnp.dot(xn, wg_ref[...], preferred_element_type=F32))
        emb = jnp.dot(p_ref[rows, :].astype(BF16), wp_ref[...], preferred_element_type=F32)
        o_ref[rows, :] = x + gate * emb


def ple(x, gain, w_gate, p, w_proj, tm_pref=512, sub_pref=256):
    t, d = x.shape
    pd = p.shape[1]
    tm = _pick(t, tm_pref)
    sub = _pick(tm, sub_pref)
    est = 4 * tm * d * 4 + 2 * d * d * 2 + 2 * tm * pd * 4 + 2 * pd * d * 2 + 6 * sub * d * 4
    return pl.pallas_call(
        functools.partial(_ple_kernel, sub=sub),
        name="ple",
        grid=(t // tm,),
        in_specs=[pl.BlockSpec((tm, d), lambda i: (i, 0)),
                  pl.BlockSpec((1, d), lambda i: (0, 0)),
                  pl.BlockSpec((d, d), lambda i: (0, 0)),
                  pl.BlockSpec((tm, pd), lambda i: (i, 0)),
                  pl.BlockSpec((pd, d), lambda i: (0, 0))],
        out_specs=pl.BlockSpec((tm, d), lambda i: (i, 0)),
        out_shape=jax.ShapeDtypeStruct((t, d), F32),
        compiler_params=_params(("parallel",), est),
    )(x, gain.reshape(1, d), w_gate, p, w_proj)


def _qk_rope_kernel(x_ref, g_ref, c_ref, s_ref, o_ref):
    y = _rms_rows(x_ref[...].astype(F32), g_ref[0])
    o_ref[...] = (y * c_ref[...] + pltpu.roll(y, HEAD_DIM // 2, axis=1) * s_ref[...]).astype(o_ref.dtype)


def qk_norm_rope(qkv, gains, cos2, sin2, seq):
    t = qkv.shape[0]
    nh = gains.shape[0]
    tm = _pick(seq, 1024)
    per_seq = seq // tm
    est = 2 * tm * HEAD_DIM * (2 + 4 + 4 + 2) + 6 * tm * HEAD_DIM * 4
    return pl.pallas_call(
        _qk_rope_kernel,
        name="qk_rope",
        grid=(t // tm, nh),
        in_specs=[pl.BlockSpec((tm, HEAD_DIM), lambda i, j: (i, j)),
                  pl.BlockSpec((1, 1, HEAD_DIM), lambda i, j: (j, 0, 0)),
                  pl.BlockSpec((tm, HEAD_DIM), lambda i, j: (i % per_seq, 0)),
                  pl.BlockSpec((tm, HEAD_DIM), lambda i, j: (i % per_seq, 0))],
        out_specs=pl.BlockSpec((tm, HEAD_DIM), lambda i, j: (i, j)),
        out_shape=jax.ShapeDtypeStruct((t, nh * HEAD_DIM), BF16),
        compiler_params=_params(("parallel", "arbitrary"), est),
    )(qkv, gains.reshape(nh, 1, HEAD_DIM), cos2, sin2)


def _attn_kernel(q_ref, k_ref, v_ref, o_ref, vt_ref, qt_ref, acc_ref, m_ref, l_ref, s_ref, *, tk, cb):
    tq = q_ref.shape[1]
    seq = k_ref.shape[1]
    r = qt_ref.shape[1]

    @pl.when(pl.program_id(2) == 0)
    def _():
        for c in range(seq // tk):
            vt_ref[c] = v_ref[0, c * tk:(c + 1) * tk, :].astype(F32).T.astype(BF16)

    q = q_ref[0]
    q2 = jnp.concatenate([q[:, :HEAD_DIM], q[:, HEAD_DIM:]], axis=0)
    qt_ref[...] = q2.astype(F32).T.astype(BF16)
    m_ref[...] = jnp.full(m_ref.shape, -jnp.inf, F32)
    l_ref[...] = jnp.zeros(l_ref.shape, F32)
    acc_ref[...] = jnp.zeros(acc_ref.shape, F32)

    blocks = [slice(b * cb, (b + 1) * cb) for b in range(r // cb)]
    n_chunks = seq // tk

    def scores(c, slot):
        start = c * tk if isinstance(c, int) else pl.multiple_of(c * tk, tk)
        kc = k_ref[0, pl.ds(start, tk), :]
        for sl in blocks:
            s_ref[slot, :, sl] = jnp.dot(kc, qt_ref[:, sl], preferred_element_type=F32)

    def softmax_pv(c, slot):
        vtc = vt_ref[c]
        s = [s_ref[slot, :, sl] for sl in blocks]
        m_old = [m_ref[:, sl] for sl in blocks]
        m_new = [jnp.maximum(mo, jnp.max(sb, axis=0, keepdims=True)) for mo, sb in zip(m_old, s)]
        alpha = [jnp.exp2(mo - mn) for mo, mn in zip(m_old, m_new)]
        p = [jnp.exp2(sb - mn) for sb, mn in zip(s, m_new)]
        for sl, a, pb, mn in zip(blocks, alpha, p, m_new):
            l_ref[:, sl] = a * l_ref[:, sl] + jnp.sum(pb, axis=0, keepdims=True)
            m_ref[:, sl] = mn
        return alpha, [jnp.dot(vtc, pb.astype(BF16), preferred_element_type=F32) for pb in p]

    def accumulate(alpha, pv):
        for sl, a, pvb in zip(blocks, alpha, pv):
            acc_ref[:, sl] = a * acc_ref[:, sl] + pvb

    def pair(c2, cur, with_next):
        c = 2 * c2
        nxt = 2 - cur
        part_a = softmax_pv(c, cur)
        if with_next:
            scores(c + 2, nxt)
        part_b = softmax_pv(c + 1, cur + 1)
        if with_next:
            scores(c + 3, nxt + 1)
        accumulate(*part_a)
        accumulate(*part_b)

    n_quads = n_chunks // 4
    scores(0, 0)
    scores(1, 1)

    def body(i, carry):
        pair(2 * i, 0, True)
        pair(2 * i + 1, 2, True)
        return carry

    lax.fori_loop(0, n_quads - 1, body, 0)
    pair(2 * n_quads - 2, 0, True)
    pair(2 * n_quads - 1, 2, False)
    o = (acc_ref[...] / l_ref[...]).T
    o_ref[0] = jnp.concatenate([o[:tq], o[tq:]], axis=1).astype(o_ref.dtype)


def attention(qk, qkv, batch, seq, tq_pref=512, tk_pref=256, cb_pref=256):
    qk3 = qk.reshape(batch, seq, qk.shape[1])
    qkv3 = qkv.reshape(batch, seq, qkv.shape[1])
    group = ATT_HEADS // ATT_KV_HEADS
    qw = group * HEAD_DIM
    tq, tk = _pick(seq, tq_pref), _pick(seq // 4, tk_pref)
    assert (seq // tk) % 4 == 0
    r = group * tq
    cb = _pick(r, cb_pref)
    k_blk0 = ATT_HEADS
    v_blk0 = ATT_HEADS + ATT_KV_HEADS
    est = (4 * tq * qw * 2 + 4 * seq * HEAD_DIM * 2 + seq * HEAD_DIM * 2 + r * HEAD_DIM * 6
           + 8 * tk * cb * 4 + 4 * r * HEAD_DIM * 4 + 4 * tk * r * 4)
    out = pl.pallas_call(
        functools.partial(_attn_kernel, tk=tk, cb=cb),
        name="attention",
        grid=(batch, ATT_KV_HEADS, seq // tq),
        in_specs=[pl.BlockSpec((1, tq, qw), lambda b, h, i: (b, i, h)),
                  pl.BlockSpec((1, seq, HEAD_DIM), lambda b, h, i: (b, 0, k_blk0 + h)),
                  pl.BlockSpec((1, seq, HEAD_DIM), lambda b, h, i: (b, 0, v_blk0 + h))],
        out_specs=pl.BlockSpec((1, tq, qw), lambda b, h, i: (b, i, h)),
        out_shape=jax.ShapeDtypeStruct((batch, seq, ATT_HEADS * HEAD_DIM), BF16),
        scratch_shapes=[pltpu.VMEM((seq // tk, HEAD_DIM, tk), BF16),
                        pltpu.VMEM((HEAD_DIM, r), BF16),
                        pltpu.VMEM((HEAD_DIM, r), F32),
                        pltpu.VMEM((1, r), F32),
                        pltpu.VMEM((1, r), F32),
                        pltpu.VMEM((4, tk, r), F32)],
        compiler_params=_params(("parallel", "parallel", "arbitrary"), est),
    )(qk3, qk3, qkv3)
    return out.reshape(batch * seq, ATT_HEADS * HEAD_DIM)


def _dn_conv_kernel(x_ref, w_ref, o_ref, xs_ref, *, rows, n_norm_blocks, q_blocks):
    seq = x_ref.shape[1]
    pad = 8
    win = rows + 2 * pad
    zeros = jnp.zeros((pad, HEAD_DIM), F32)
    xs_ref[0:pad, :] = zeros
    xs_ref[seq + pad:seq + 2 * pad, :] = zeros
    xs_ref[pad:seq + pad, :] = x_ref[0].astype(F32)
    w = w_ref[...]
    j = pl.program_id(1)
    q_scale = jnp.where(j < q_blocks, HEAD_DIM ** -0.5, 1.0).astype(F32)

    def conv_silu(r):
        base = pl.multiple_of(r * rows, rows)
        acc = jnp.zeros((rows, HEAD_DIM), F32)
        for tap in range(DN_CONV_W):
            shifted = xs_ref[pl.ds(base + (pad - DN_CONV_W // 2 + tap), rows), :]
            acc = acc + w[tap:tap + 1, :] * shifted
        return base, jax.nn.silu(acc)

    @pl.when(j < n_norm_blocks)
    def _():
        def body(r, carry):
            base, y = conv_silu(r)
            ss = jnp.sum(y * y, axis=-1, keepdims=True)
            o_ref[0, pl.ds(base, rows), :] = (y * lax.rsqrt(ss + NORM_EPS) * q_scale).astype(o_ref.dtype)
            return carry
        lax.fori_loop(0, seq // rows, body, 0)

    @pl.when(j >= n_norm_blocks)
    def _():
        def body(r, carry):
            base, y = conv_silu(r)
            o_ref[0, pl.ds(base, rows), :] = y.astype(o_ref.dtype)
            return carry
        lax.fori_loop(0, seq // rows, body, 0)


def dn_conv(proj3, conv_w):
    batch, seq, _ = proj3.shape
    width = conv_w.shape[1]
    nblk = width // HEAD_DIM
    rows = _pick(seq, 1024)
    w8 = jnp.zeros((8, width), F32).at[:DN_CONV_W].set(conv_w.astype(F32))
    est = 4 * seq * HEAD_DIM * 2 + (seq + 16) * HEAD_DIM * 4 + 16 * rows * HEAD_DIM * 4
    return pl.pallas_call(
        functools.partial(_dn_conv_kernel, rows=rows, n_norm_blocks=2 * DN_K_HEADS, q_blocks=DN_K_HEADS),
        name="dn_conv",
        grid=(batch, nblk),
        in_specs=[pl.BlockSpec((1, seq, HEAD_DIM), lambda b, j: (b, 0, j)),
                  pl.BlockSpec((8, HEAD_DIM), lambda b, j: (0, j))],
        out_specs=pl.BlockSpec((1, seq, HEAD_DIM), lambda b, j: (b, 0, j)),
        out_shape=jax.ShapeDtypeStruct((batch, seq, width), BF16),
        scratch_shapes=[pltpu.VMEM((seq + 16, HEAD_DIM), F32)],
        compiler_params=_params(("parallel", "arbitrary"), est),
    )(proj3, w8)


def _dn_gates_kernel(ba_ref, alog_ref, dtb_ref, col_ref, row_ref):
    ba = ba_ref[0]
    tg = ba.shape[0]
    lane = lax.broadcasted_iota(jnp.int32, ba.shape, 1)
    pos = lax.broadcasted_iota(jnp.int32, ba.shape, 0) & (DN_CHUNK - 1)
    beta = jax.nn.sigmoid(ba)
    g = -jnp.exp(alog_ref[...]) * jax.nn.softplus(ba + dtb_ref[...])
    g = jnp.where(lane >= 2 * DN_V_HEADS, g, 0.0)
    pre = g
    suf = g
    k = 1
    while k < DN_CHUNK:
        pre = pre + jnp.where(pos >= k, pltpu.roll(pre, k, axis=0), 0.0)
        suf = suf + jnp.where(pos < DN_CHUNK - k, pltpu.roll(suf, tg - k, axis=0), 0.0)
        k *= 2
    out = jnp.where(lane < 2 * DN_V_HEADS, beta, jnp.where(lane < 3 * DN_V_HEADS, pre, suf))
    col_ref[0] = out
    row_ref[0] = out.T


def dn_gates(ba3, a_log, dt_bias):
    batch, seq, w = ba3.shape
    tg = _pick(seq, 256)
    zeros = jnp.zeros((2 * DN_V_HEADS,), F32)
    alog_row = jnp.concatenate([zeros, a_log.astype(F32).reshape(-1)]).reshape(1, w)
    dtb_row = jnp.concatenate([zeros, dt_bias.astype(F32).reshape(-1)]).reshape(1, w)
    est = 24 * tg * w * 4
    return pl.pallas_call(
        _dn_gates_kernel,
        name="dn_gates",
        grid=(batch, seq // tg),
        in_specs=[pl.BlockSpec((1, tg, w), lambda b, i: (b, i, 0)),
                  pl.BlockSpec((1, w), lambda b, i: (0, 0)),
                  pl.BlockSpec((1, w), lambda b, i: (0, 0))],
        out_specs=[pl.BlockSpec((1, tg, w), lambda b, i: (b, i, 0)),
                   pl.BlockSpec((1, w, tg), lambda b, i: (b, 0, i))],
        out_shape=[jax.ShapeDtypeStruct((batch, seq, w), F32),
                   jax.ShapeDtypeStruct((batch, w, seq), F32)],
        compiler_params=_params(("parallel", "arbitrary"), est),
    )(ba3, alog_row, dtb_row)


def _pair_masks(backward):
    c = DN_CHUNK
    i = lax.broadcasted_iota(jnp.int32, (c, 2 * c), 0)
    lane = lax.broadcasted_iota(jnp.int32, (c, 2 * c), 1)
    j = lane & (c - 1)
    lo = lane < c
    if backward:
        i, j = j, i
    levels = []
    s = 1
    while s < c:
        sh = int(math.log2(2 * s))
        same = lax.shift_right_logical(i, sh) == lax.shift_right_logical(j, sh)
        levels.append(same & ((i & (2 * s - 1)) >= s) & ((j & (2 * s - 1)) < s))
        s *= 2
    return lo, i == j, i >= j, i > j, levels


def _block_diag2(x, lo):
    return jnp.concatenate([jnp.where(lo, x, 0.0), jnp.where(lo, 0.0, x)], axis=0).astype(BF16)


def _block_diag_wide(x):
    half = x.shape[1] // 2
    z = jnp.zeros((x.shape[0], half), x.dtype)
    return jnp.concatenate([jnp.concatenate([x[:, :half], z], axis=1),
                            jnp.concatenate([z, x[:, half:]], axis=1)], axis=0)


def _dn_chunk_setup(backward, hk, kh, q_ref, k_ref, v_ref, col_ref, row_ref, o_ref):
    c = DN_CHUNK
    nt = (((1,), (1,)), ((), ()))
    n_chunks = q_ref.shape[1] // c
    lo, eye, incl, strict, levels = _pair_masks(backward)
    lo_row = lo[0:1]
    cols = col_ref[0]
    lane = lax.broadcasted_iota(jnp.int32, cols.shape, 1)
    dir_off = DN_V_HEADS if backward else 0
    beta_lane = dir_off + 2 * hk
    g_lane = 2 * DN_V_HEADS + dir_off + 2 * hk

    def column(ln):
        return jnp.sum(jnp.where(lane == ln, cols, 0.0), axis=1, keepdims=True)

    beta_cols = (column(beta_lane), column(beta_lane + 1))
    gc_cols = (column(g_lane), column(g_lane + 1))
    gc_rows = (row_ref[0, pl.ds(g_lane, 1), :], row_ref[0, pl.ds(g_lane + 1, 1), :])
    zero = jnp.zeros((c, HEAD_DIM), F32)
    chunks = []
    for ci in range(n_chunks):
        r0 = ci * c
        v0 = (r0 // (2 * c)) * 2 * c
        row_a = gc_rows[0][:, v0:v0 + 2 * c]
        row_b = gc_rows[1][:, v0:v0 + 2 * c]
        if r0 == v0:
            grow = jnp.where(lo_row, row_a, pltpu.roll(row_b, c, axis=1))
        else:
            grow = jnp.where(lo_row, pltpu.roll(row_a, c, axis=1), row_b)
        bcol = [x[r0:r0 + c] for x in beta_cols]
        gcol = [x[r0:r0 + c] for x in gc_cols]
        g_last = [g[0:1] if backward else g[c - 1:c] for g in gcol]
        decay = jnp.exp(jnp.where(incl, jnp.where(lo, gcol[0], gcol[1]) - grow, -jnp.inf))
        kc = k_ref[0, r0:r0 + c, kh * HEAD_DIM:(kh + 1) * HEAD_DIM]
        qc = q_ref[0, r0:r0 + c, kh * HEAD_DIM:(kh + 1) * HEAD_DIM]
        kf = kc.astype(F32)
        qf = qc.astype(F32)
        vp = v_ref[0, r0:r0 + c, 2 * kh * HEAD_DIM:2 * (kh + 1) * HEAD_DIM].astype(F32)
        kb = [kf * (bcol[e] * jnp.exp(gcol[e])) for e in range(2)]
        vb = [vp[:, e * HEAD_DIM:(e + 1) * HEAD_DIM] * bcol[e] for e in range(2)]
        rhs = jnp.concatenate([jnp.concatenate([vb[0], zero, kb[0], zero], axis=1),
                               jnp.concatenate([zero, vb[1], zero, kb[1]], axis=1)], axis=0).astype(BF16)
        q_dec = jnp.concatenate([qf * jnp.exp(gcol[0]), qf * jnp.exp(gcol[1])], axis=1).astype(BF16)
        k_dec = jnp.concatenate([kf * jnp.exp(g_last[e] - gcol[e]) for e in range(2)], axis=0)
        e_last = jnp.concatenate([jnp.broadcast_to(jnp.exp(g_last[e]), (1, HEAD_DIM)) for e in range(2)], axis=1)
        kc2 = jnp.concatenate([kc, kc], axis=0)
        chunks.append(dict(
            r0=r0, o_ref=o_ref, o_cols=slice(2 * kh * HEAD_DIM, 2 * (kh + 1) * HEAD_DIM), lo=lo, eye=eye, incl=incl, strict=strict, levels=levels,
            beta=jnp.where(lo, bcol[0], bcol[1]), decay=decay, rhs=rhs, q_dec=q_dec,
            k_dec_t=k_dec.T.astype(BF16), e_last=e_last,
            kk=lax.dot_general(kc, kc2, nt, preferred_element_type=F32),
            qk=lax.dot_general(qc, kc2, nt, preferred_element_type=F32)))
    return chunks


def _dn_core_kernel(qf_ref, kf_ref, vf_ref, colf_ref, rowf_ref,
                    qb_ref, kb_ref, vb_ref, colb_ref, rowb_ref,
                    of_ref, ob_ref, st_ref):
    @pl.when(pl.program_id(2) == 0)
    def _():
        st_ref[...] = jnp.zeros(st_ref.shape, F32)

    c = DN_CHUNK
    nk = qf_ref.shape[2] // HEAD_DIM
    hk0 = pl.program_id(1) * nk
    chains = ([_dn_chunk_setup(False, hk0 + kh, kh, qf_ref, kf_ref, vf_ref, colf_ref, rowf_ref, of_ref)
               for kh in range(nk)]
              + [_dn_chunk_setup(True, hk0 + kh, kh, qb_ref, kb_ref, vb_ref, colb_ref, rowb_ref, ob_ref)
                 for kh in range(nk)])
    insts = [it for chain in chains for it in chain]
    n_levels = len(insts[0]['levels'])

    for it in insts:
        lmat = jnp.where(it['strict'], it['beta'] * it['kk'] * it['decay'], 0.0)
        it['lmat'] = lmat
        it['t'] = jnp.where(it['eye'], 1.0, 0.0).astype(F32) - jnp.where(it['levels'][0], lmat, 0.0)
        it['intra'] = jnp.where(it['incl'], it['qk'] * it['decay'], 0.0).astype(BF16)
    for lv in range(1, n_levels):
        for it in insts:
            off = _block_diag2(jnp.where(it['levels'][lv], it['lmat'], 0.0), it['lo'])
            it['x'] = jnp.dot(it['t'].astype(BF16), off, preferred_element_type=F32).astype(BF16)
        for it in insts:
            it['t'] = it['t'] - jnp.dot(it['x'], _block_diag2(it['t'], it['lo']), preferred_element_type=F32)
    for it in insts:
        sol = jnp.dot(it['t'].astype(BF16), it['rhs'], preferred_element_type=F32)
        it['u'] = sol[:, :2 * HEAD_DIM]
        it['lhs_state'] = jnp.concatenate([sol[:, 2 * HEAD_DIM:].astype(BF16), it['q_dec']], axis=0)
        it['lhs_vnew'] = jnp.concatenate([it['intra'], it['k_dec_t']], axis=0)

    n_chunks = len(chains[0])
    states = [st_ref[i] for i in range(2 * nk)]
    for s in range(n_chunks):
        step = [chain[s] if i < nk else chain[n_chunks - 1 - s] for i, chain in enumerate(chains)]
        from_state = [jnp.dot(it['lhs_state'], _block_diag_wide(states[d].astype(BF16)),
                              preferred_element_type=F32) for d, it in enumerate(step)]
        v_new = [(it['u'] - from_state[d][:c]).astype(BF16) for d, it in enumerate(step)]
        from_vnew = [jnp.dot(it['lhs_vnew'], _block_diag_wide(v_new[d]), preferred_element_type=F32)
                     for d, it in enumerate(step)]
        for d, it in enumerate(step):
            o = from_state[d][c:] + from_vnew[d][:c]
            it['o_ref'][0, it['r0']:it['r0'] + c, it['o_cols']] = o.astype(it['o_ref'].dtype)
            states[d] = states[d] * it['e_last'] + from_vnew[d][c:]
    for i in range(2 * nk):
        st_ref[i] = states[i]


def dn_core(qkv_c, gcol, grow, gt_pref=512, nk=4):
    batch, seq, _ = qkv_c.shape
    gt = _pick(seq, gt_pref)
    ng = seq // gt
    kw = nk * HEAD_DIM
    vw = 2 * kw
    n_steps = DN_K_HEADS // nk
    assert n_steps * nk == DN_K_HEADS
    k_blk0 = n_steps
    v_blk0 = n_steps
    gw = gcol.shape[2]

    def fwd(spec_cols):
        return lambda b, h, c: (b, c, spec_cols(h))

    def bwd(spec_cols):
        return lambda b, h, c: (b, ng - 1 - c, spec_cols(h))

    def specs(order):
        return [pl.BlockSpec((1, gt, kw), order(lambda h: h)),
                pl.BlockSpec((1, gt, kw), order(lambda h: k_blk0 + h)),
                pl.BlockSpec((1, gt, vw), order(lambda h: v_blk0 + h)),
                pl.BlockSpec((1, gt, gw), order(lambda h: 0)),
                pl.BlockSpec((1, gw, gt), (lambda b, h, c: (b, 0, c)) if order is fwd
                             else (lambda b, h, c: (b, 0, ng - 1 - c)))]

    out_sd = jax.ShapeDtypeStruct((batch, seq, DN_V_HEADS * HEAD_DIM), BF16)
    est = 4 * gt * (2 * kw * 2 + vw * 2 + 2 * gw * 4) + 4 * gt * vw * 2 + nk * 64 * gt * HEAD_DIM * 4
    return pl.pallas_call(
        _dn_core_kernel,
        name="dn_core",
        grid=(batch, n_steps, ng),
        in_specs=specs(fwd) + specs(bwd),
        out_specs=[pl.BlockSpec((1, gt, vw), lambda b, h, c: (b, c, h)),
                   pl.BlockSpec((1, gt, vw), lambda b, h, c: (b, ng - 1 - c, h))],
        out_shape=[out_sd, out_sd],
        scratch_shapes=[pltpu.VMEM((2 * nk, HEAD_DIM, 2 * HEAD_DIM), F32)],
        compiler_params=_params(("parallel", "parallel", "arbitrary"), est),
    )(qkv_c, qkv_c, qkv_c, gcol, grow, qkv_c, qkv_c, qkv_c, gcol, grow)


def _dn_out_kernel(of_ref, ob_ref, z_ref, g_ref, w_ref, r_ref, o_ref, a_ref):
    @pl.when(pl.program_id(1) == 0)
    def _():
        gain = g_ref[...]
        for h in range(a_ref.shape[1] // HEAD_DIM):
            sl = slice(h * HEAD_DIM, (h + 1) * HEAD_DIM)
            o = of_ref[:, sl].astype(F32) + ob_ref[:, sl].astype(F32)
            y = _rms_rows(o, gain) * jax.nn.silu(z_ref[:, sl].astype(F32))
            a_ref[:, sl] = y.astype(BF16)

    o_ref[...] = r_ref[...] + jnp.dot(a_ref[...], w_ref[...], preferred_element_type=F32)


def dn_out(o_f, o_b, proj, z_col0, out_gain, w_out, res, tm_pref=512, tn_pref=512):
    t, kdim = o_f.shape
    n = w_out.shape[1]
    tm, tn = _pick(t, tm_pref), _pick(n, tn_pref)
    z_blk = z_col0 // kdim
    assert z_col0 % kdim == 0
    est = 6 * tm * kdim * 2 + tm * kdim * 2 + 2 * kdim * tn * 2 + 4 * tm * tn * 4 + 8 * tm * HEAD_DIM * 4
    return pl.pallas_call(
        _dn_out_kernel,
        name="dn_out",
        grid=(t // tm, n // tn),
        in_specs=[pl.BlockSpec((tm, kdim), lambda i, j: (i, 0)),
                  pl.BlockSpec((tm, kdim), lambda i, j: (i, 0)),
                  pl.BlockSpec((tm, kdim), lambda i, j: (i, z_blk)),
                  pl.BlockSpec((1, HEAD_DIM), lambda i, j: (0, 0)),
                  pl.BlockSpec((kdim, tn), lambda i, j: (0, j)),
                  pl.BlockSpec((tm, tn), lambda i, j: (i, j))],
        out_specs=pl.BlockSpec((tm, tn), lambda i, j: (i, j)),
        out_shape=jax.ShapeDtypeStruct((t, n), F32),
        scratch_shapes=[pltpu.VMEM((tm, kdim), BF16)],
        compiler_params=_params(("parallel", "arbitrary"), est),
    )(o_f, o_b, proj, out_gain.reshape(1, HEAD_DIM), w_out, res)


def _deltanet_layer(h, batch, seq, norm_gain, w_in_main, w_in_ba, conv_w, a_log, dt_bias, out_gain, w_out):
    proj = norm_matmul(h, norm_gain, w_in_main, BF16)
    ba = norm_matmul(h, norm_gain, w_in_ba, F32, tn_pref=128)
    qkv_c = dn_conv(proj.reshape(batch, seq, -1), conv_w)
    gcol, grow = dn_gates(ba.reshape(batch, seq, -1), a_log, dt_bias)
    o_f, o_b = dn_core(qkv_c, gcol, grow)
    vwidth = DN_V_HEADS * HEAD_DIM
    return dn_out(o_f.reshape(batch * seq, vwidth), o_b.reshape(batch * seq, vwidth), proj,
                  conv_w.shape[1], out_gain, w_out, h)


def _attention_layer(h, batch, seq, norm_gain, w_in, qk_gains, w_out, cos2, sin2):
    qkv = norm_matmul(h, norm_gain, w_in, BF16)
    qk = qk_norm_rope(qkv, qk_gains, cos2, sin2, seq)
    o = attention(qk, qkv, batch, seq)
    return matmul_res(o, w_out, h)


def _rope_tables(n):
    rows = n // GRID_W
    row = jnp.repeat(jnp.arange(rows, dtype=F32), GRID_W)
    col = jnp.tile(jnp.arange(GRID_W, dtype=F32), rows)
    half = HEAD_DIM // 2
    inv_freq = ROPE_THETA ** (-jnp.arange(0, half, 2, dtype=F32) / half)
    ang = jnp.concatenate([row[:, None] * inv_freq, col[:, None] * inv_freq], axis=-1)
    cos, sin = jnp.cos(ang), jnp.sin(ang)
    return jnp.concatenate([cos, cos], axis=-1), jnp.concatenate([-sin, sin], axis=-1)


def _deinterleave_heads(n_heads):
    within = jnp.concatenate([jnp.arange(0, HEAD_DIM, 2), jnp.arange(1, HEAD_DIM, 2)])
    return (jnp.arange(n_heads)[:, None] * HEAD_DIM + within[None, :]).reshape(-1)


def _prepare_weights(dn_w_in, dn_w_out, at_w_in, at_q_norm, at_k_norm, at_w_out, mlp_w_up, mlp_w_down,
                     ple_w_gate, ple_w_proj):
    qkvz = dn_w_in.shape[2] - 4 * DN_V_HEADS
    n_qk = ATT_HEADS + ATT_KV_HEADS
    perm = jnp.concatenate([_deinterleave_heads(n_qk),
                            jnp.arange(n_qk * HEAD_DIM, at_w_in.shape[2])])
    within = _deinterleave_heads(1)
    scale = HEAD_DIM ** -0.5 * math.log2(math.e)
    qk_gains = jnp.concatenate(
        [jnp.broadcast_to((at_q_norm.astype(F32) * scale)[:, None, within], (at_q_norm.shape[0], ATT_HEADS, HEAD_DIM)),
         jnp.broadcast_to(at_k_norm.astype(F32)[:, None, within], (at_k_norm.shape[0], ATT_KV_HEADS, HEAD_DIM))],
        axis=1)
    return dict(
        dn_w_in_main=dn_w_in[:, :, :qkvz].astype(BF16),
        dn_w_in_ba=dn_w_in[:, :, qkvz:].astype(BF16),
        dn_w_out=dn_w_out.astype(BF16),
        at_w_in=at_w_in[:, :, perm].astype(BF16),
        qk_gains=qk_gains,
        at_w_out=at_w_out.astype(BF16),
        mlp_w_up=mlp_w_up.astype(BF16),
        mlp_w_down=mlp_w_down.astype(BF16),
        ple_w_gate=ple_w_gate.astype(BF16),
        ple_w_proj=ple_w_proj.astype(BF16),
    )


def _trunk(x, p, wts, norm_mix, norm_mlp, dn_conv_w, dn_a_log, dn_dt_bias, dn_out_norm, ple_norm):
    batch, seq, d = x.shape
    depth = p.shape[0]
    cos2, sin2 = _rope_tables(seq)
    h = x.reshape(batch * seq, d)
    for i in range(depth):
        j = i // 2
        if i % 2 == 0:
            h = _deltanet_layer(h, batch, seq, norm_mix[i], wts['dn_w_in_main'][j], wts['dn_w_in_ba'][j],
                                dn_conv_w[j], dn_a_log[j], dn_dt_bias[j], dn_out_norm[j], wts['dn_w_out'][j])
        else:
            h = _attention_layer(h, batch, seq, norm_mix[i], wts['at_w_in'][j], wts['qk_gains'][j],
                                 wts['at_w_out'][j], cos2, sin2)
        h = mlp(h, norm_mlp[i], wts['mlp_w_up'][i], wts['mlp_w_down'][i])
        h = ple(h, ple_norm[i], wts['ple_w_gate'][i], p[i].reshape(batch * seq, -1), wts['ple_w_proj'][i])
    return h.reshape(batch, seq, d)


def kernel(x_prompt, x_sample, p_prompt, p_sample, norm_mix, norm_mlp, dn_w_in, dn_conv, dn_a_log, dn_dt_bias, dn_out_norm, dn_w_out, at_w_in, at_q_norm, at_k_norm, at_w_out, mlp_w_up, mlp_w_down, ple_norm, ple_w_gate, ple_w_proj):
    wts = _prepare_weights(dn_w_in, dn_w_out, at_w_in, at_q_norm, at_k_norm, at_w_out, mlp_w_up, mlp_w_down,
                           ple_w_gate, ple_w_proj)
    shared = (wts, norm_mix, norm_mlp, dn_conv, dn_a_log, dn_dt_bias, dn_out_norm, ple_norm)
    return (_trunk(x_prompt, p_prompt, *shared), _trunk(x_sample, p_sample, *shared))
```

```python
import functools
import math

import jax
import jax.numpy as jnp
from jax import lax
from jax.experimental import pallas as pl
from jax.experimental.pallas import tpu as pltpu

F32 = jnp.float32
BF16 = jnp.bfloat16

NORM_EPS = 1e-6
GRID_W = 64
ROPE_THETA = 10000.0
HEAD_DIM = 128
DN_K_HEADS = 16
DN_V_HEADS = 32
DN_CONV_W = 5
DN_CHUNK = 64
ATT_HEADS = 16
ATT_KV_HEADS = 8

V7X_VMEM_BUDGET = 60000 * 1024
COMPILER_SCRATCH = 8 * 1024 * 1024
BF16_ROWS_PER_VREG = 16


def _pick(n, pref):
    t = pref
    while t > 8 and n % t:
        t //= 2
    assert n % t == 0, (n, pref)
    return t


def _params(semantics, est_bytes):
    limit = int(min(V7X_VMEM_BUDGET, est_bytes + COMPILER_SCRATCH))
    return pltpu.CompilerParams(dimension_semantics=semantics, vmem_limit_bytes=limit)


def _rms_rows(x, gain):
    ms = jnp.mean(x * x, axis=-1, keepdims=True)
    return x * lax.rsqrt(ms + NORM_EPS) * gain


def _norm_matmul_kernel(x_ref, g_ref, w_ref, o_ref, xn_ref):
    @pl.when(pl.program_id(1) == 0)
    def _():
        xn_ref[...] = _rms_rows(x_ref[...], g_ref[...]).astype(BF16)

    o_ref[...] = jnp.dot(xn_ref[...], w_ref[...], preferred_element_type=F32).astype(o_ref.dtype)


def norm_matmul(x, gain, w, out_dtype, tm_pref=1024, tn_pref=1024):
    t, d = x.shape
    n = w.shape[1]
    tm, tn = _pick(t, tm_pref), _pick(n, tn_pref)
    osz = jnp.dtype(out_dtype).itemsize
    est = 2 * tm * d * 4 + tm * d * 2 + 2 * d * tn * 2 + 2 * tm * tn * osz + tm * tn * 4
    return pl.pallas_call(
        _norm_matmul_kernel,
        name="norm_matmul",
        grid=(t // tm, n // tn),
        in_specs=[pl.BlockSpec((tm, d), lambda i, j: (i, 0)),
                  pl.BlockSpec((1, d), lambda i, j: (0, 0)),
                  pl.BlockSpec((d, tn), lambda i, j: (0, j))],
        out_specs=pl.BlockSpec((tm, tn), lambda i, j: (i, j)),
        out_shape=jax.ShapeDtypeStruct((t, n), out_dtype),
        scratch_shapes=[pltpu.VMEM((tm, d), BF16)],
        compiler_params=_params(("parallel", "arbitrary"), est),
    )(x, gain.reshape(1, d), w)


def _matmul_res_kernel(a_ref, w_ref, r_ref, o_ref):
    o_ref[...] = r_ref[...] + jnp.dot(a_ref[...], w_ref[...], preferred_element_type=F32)


def matmul_res(a, w, res, tm_pref=1024, tn_pref=1024):
    t, k = a.shape
    n = w.shape[1]
    tm, tn = _pick(t, tm_pref), _pick(n, tn_pref)
    est = 2 * tm * k * 2 + 2 * k * tn * 2 + 4 * tm * tn * 4 + tm * tn * 4
    return pl.pallas_call(
        _matmul_res_kernel,
        name="matmul_res",
        grid=(t // tm, n // tn),
        in_specs=[pl.BlockSpec((tm, k), lambda i, j: (i, 0)),
                  pl.BlockSpec((k, tn), lambda i, j: (0, j)),
                  pl.BlockSpec((tm, tn), lambda i, j: (i, j))],
        out_specs=pl.BlockSpec((tm, tn), lambda i, j: (i, j)),
        out_shape=jax.ShapeDtypeStruct((t, n), F32),
        compiler_params=_params(("parallel", "arbitrary"), est),
    )(a, w, res)


def _mlp_kernel(x_ref, g_ref, wu_ref, wd_ref, o_ref, xn_ref):
    @pl.when(pl.program_id(1) == 0)
    def _():
        x = x_ref[...]
        xn_ref[...] = _rms_rows(x, g_ref[...]).astype(BF16)
        o_ref[...] = x

    hid = jnp.maximum(jnp.dot(xn_ref[...], wu_ref[...], preferred_element_type=F32), 0.0)
    hid = (hid * hid).astype(BF16)
    o_ref[...] += jnp.dot(hid, wd_ref[...], preferred_element_type=F32)


def mlp(x, gain, w_up, w_down, tm_pref=512, tf_pref=1024):
    t, d = x.shape
    f = w_up.shape[1]
    tm, tf = _pick(t, tm_pref), _pick(f, tf_pref)
    est = 4 * tm * d * 4 + tm * d * 2 + 4 * d * tf * 2 + tm * tf * 6 + tm * d * 4
    return pl.pallas_call(
        _mlp_kernel,
        name="mlp",
        grid=(t // tm, f // tf),
        in_specs=[pl.BlockSpec((tm, d), lambda i, j: (i, 0)),
                  pl.BlockSpec((1, d), lambda i, j: (0, 0)),
                  pl.BlockSpec((d, tf), lambda i, j: (0, j)),
                  pl.BlockSpec((tf, d), lambda i, j: (j, 0))],
        out_specs=pl.BlockSpec((tm, d), lambda i, j: (i, 0)),
        out_shape=jax.ShapeDtypeStruct((t, d), F32),
        scratch_shapes=[pltpu.VMEM((tm, d), BF16)],
        compiler_params=_params(("parallel", "arbitrary"), est),
    )(x, gain.reshape(1, d), w_up, w_down)


def _ple_kernel(x_ref, g_ref, wg_ref, p_ref, wp_ref, o_ref, *, sub):
    for s0 in range(0, x_ref.shape[0], sub):
        rows = slice(s0, s0 + sub)
        x = x_ref[rows, :]
        xn = _rms_rows(x, g_ref[...]).astype(BF16)
        gate = jax.nn.sigmoid(jnp.dot(xn, wg_ref[...], preferred_element_type=F32))
        emb = jnp.dot(p_ref[rows, :].astype(BF16), wp_ref[...], preferred_element_type=F32)
        o_ref[rows, :] = x + gate * emb


def ple(x, gain, w_gate, p, w_proj, tm_pref=512, sub_pref=256):
    t, d = x.shape
    pd = p.shape[1]
    tm = _pick(t, tm_pref)
    sub = _pick(tm, sub_pref)
    est = 4 * tm * d * 4 + 2 * d * d * 2 + 2 * tm * pd * 4 + 2 * pd * d * 2 + 6 * sub * d * 4
    return pl.pallas_call(
        functools.partial(_ple_kernel, sub=sub),
        name="ple",
        grid=(t // tm,),
        in_specs=[pl.BlockSpec((tm, d), lambda i: (i, 0)),
                  pl.BlockSpec((1, d), lambda i: (0, 0)),
                  pl.BlockSpec((d, d), lambda i: (0, 0)),
                  pl.BlockSpec((tm, pd), lambda i: (i, 0)),
                  pl.BlockSpec((pd, d), lambda i: (0, 0))],
        out_specs=pl.BlockSpec((tm, d), lambda i: (i, 0)),
        out_shape=jax.ShapeDtypeStruct((t, d), F32),
        compiler_params=_params(("parallel",), est),
    )(x, gain.reshape(1, d), w_gate, p, w_proj)


def _qk_rope_kernel(x_ref, g_ref, c_ref, s_ref, o_ref):
    cos, sin = c_ref[...], s_ref[...]
    for h in range(g_ref.shape[0]):
        sl = slice(h * HEAD_DIM, (h + 1) * HEAD_DIM)
        y = _rms_rows(x_ref[:, sl].astype(F32), g_ref[h])
        o_ref[:, sl] = (y * cos + pltpu.roll(y, HEAD_DIM // 2, axis=1) * sin).astype(o_ref.dtype)


def qk_norm_rope(qkv, gains, cos2, sin2, seq, hb_pref=8):
    t = qkv.shape[0]
    nh = gains.shape[0]
    hb = _pick(nh, hb_pref)
    tm = _pick(seq, 512)
    per_seq = seq // tm
    est = 4 * tm * hb * HEAD_DIM * 2 + 4 * tm * HEAD_DIM * 4 + 8 * tm * HEAD_DIM * 4
    return pl.pallas_call(
        _qk_rope_kernel,
        name="qk_rope",
        grid=(t // tm, nh // hb),
        in_specs=[pl.BlockSpec((tm, hb * HEAD_DIM), lambda i, j: (i, j)),
                  pl.BlockSpec((hb, 1, HEAD_DIM), lambda i, j: (j, 0, 0)),
                  pl.BlockSpec((tm, HEAD_DIM), lambda i, j: (i % per_seq, 0)),
                  pl.BlockSpec((tm, HEAD_DIM), lambda i, j: (i % per_seq, 0))],
        out_specs=pl.BlockSpec((tm, hb * HEAD_DIM), lambda i, j: (i, j)),
        out_shape=jax.ShapeDtypeStruct((t, nh * HEAD_DIM), BF16),
        compiler_params=_params(("parallel", "arbitrary"), est),
    )(qkv, gains.reshape(nh, 1, HEAD_DIM), cos2, sin2)


PAIRS_PER_TRIP = 4


def _attn_kernel(q_ref, k_ref, v_ref, o_ref, vt_ref, qt_ref, acc_ref, m_ref, s_ref, *, tk, cb):
    tq = q_ref.shape[1]
    seq = k_ref.shape[1]
    r = qt_ref.shape[1]

    @pl.when(pl.program_id(2) == 0)
    def _():
        ones = jnp.ones((vt_ref.shape[1] - HEAD_DIM, tk), BF16)
        for c in range(seq // tk):
            vt_ref[c, :HEAD_DIM, :] = v_ref[0, c * tk:(c + 1) * tk, :].astype(F32).T.astype(BF16)
            vt_ref[c, HEAD_DIM:, :] = ones

    q = q_ref[0]
    q2 = jnp.concatenate([q[:, :HEAD_DIM], q[:, HEAD_DIM:]], axis=0)
    qt_ref[...] = q2.astype(F32).T.astype(BF16)
    m_ref[...] = jnp.full(m_ref.shape, -jnp.inf, F32)
    acc_ref[...] = jnp.zeros(acc_ref.shape, F32)

    blocks = [slice(b * cb, (b + 1) * cb) for b in range(r // cb)]
    n_chunks = seq // tk

    def scores(c, slot):
        start = c * tk if isinstance(c, int) else pl.multiple_of(c * tk, tk)
        kc = k_ref[0, pl.ds(start, tk), :]
        for sl in blocks:
            s_ref[slot, :, sl] = jnp.dot(kc, qt_ref[:, sl], preferred_element_type=F32)

    def softmax_pv(c, slot):
        vtc = vt_ref[c]
        s = [s_ref[slot, :, sl] for sl in blocks]
        m_old = [m_ref[:, sl] for sl in blocks]
        m_new = [jnp.maximum(mo, jnp.max(sb, axis=0, keepdims=True)) for mo, sb in zip(m_old, s)]
        alpha = [jnp.exp2(mo - mn) for mo, mn in zip(m_old, m_new)]
        p = [jnp.exp2(sb - mn) for sb, mn in zip(s, m_new)]
        for sl, mn in zip(blocks, m_new):
            m_ref[:, sl] = mn
        return alpha, [jnp.dot(vtc, pb.astype(BF16), preferred_element_type=F32) for pb in p]

    def accumulate(alpha, pv):
        for sl, a, pvb in zip(blocks, alpha, pv):
            acc_ref[:, sl] = a * acc_ref[:, sl] + pvb

    def pair(c2, cur, with_next):
        c = 2 * c2
        nxt = 2 - cur
        if with_next:
            scores(c + 2, nxt)
        part_a = softmax_pv(c, cur)
        if with_next:
            scores(c + 3, nxt + 1)
        part_b = softmax_pv(c + 1, cur + 1)
        accumulate(*part_a)
        accumulate(*part_b)

    def trip(i, last):
        for j in range(PAIRS_PER_TRIP):
            pair(PAIRS_PER_TRIP * i + j, 2 * (j % 2), not (last and j == PAIRS_PER_TRIP - 1))

    n_trips = n_chunks // (2 * PAIRS_PER_TRIP)
    scores(0, 0)
    scores(1, 1)

    def body(i, carry):
        trip(i, False)
        return carry

    lax.fori_loop(0, n_trips - 1, body, 0)
    trip(n_trips - 1, True)
    o = (acc_ref[:HEAD_DIM, :] / acc_ref[HEAD_DIM:HEAD_DIM + 1, :]).T
    o_ref[0] = jnp.concatenate([o[:tq], o[tq:]], axis=1).astype(o_ref.dtype)


def attention(qk, qkv, batch, seq, tq_pref=1024, tk_pref=256, cb_pref=256):
    qk3 = qk.reshape(batch, seq, qk.shape[1])
    qkv3 = qkv.reshape(batch, seq, qkv.shape[1])
    group = ATT_HEADS // ATT_KV_HEADS
    qw = group * HEAD_DIM
    tq, tk = _pick(seq, tq_pref), _pick(seq // (2 * PAIRS_PER_TRIP), tk_pref)
    assert (seq // tk) % (2 * PAIRS_PER_TRIP) == 0
    r = group * tq
    cb = _pick(r, cb_pref)
    k_blk0 = ATT_HEADS
    v_blk0 = ATT_HEADS + ATT_KV_HEADS
    est = (4 * tq * qw * 2 + 4 * seq * HEAD_DIM * 2 + seq * HEAD_DIM * 2 + r * HEAD_DIM * 6
           + 8 * tk * cb * 4 + 4 * r * HEAD_DIM * 4 + 4 * tk * r * 4)
    out = pl.pallas_call(
        functools.partial(_attn_kernel, tk=tk, cb=cb),
        name="attention",
        grid=(batch, ATT_KV_HEADS, seq // tq),
        in_specs=[pl.BlockSpec((1, tq, qw), lambda b, h, i: (b, i, h)),
                  pl.BlockSpec((1, seq, HEAD_DIM), lambda b, h, i: (b, 0, k_blk0 + h)),
                  pl.BlockSpec((1, seq, HEAD_DIM), lambda b, h, i: (b, 0, v_blk0 + h))],
        out_specs=pl.BlockSpec((1, tq, qw), lambda b, h, i: (b, i, h)),
        out_shape=jax.ShapeDtypeStruct((batch, seq, ATT_HEADS * HEAD_DIM), BF16),
        scratch_shapes=[pltpu.VMEM((seq // tk, HEAD_DIM + BF16_ROWS_PER_VREG, tk), BF16),
                        pltpu.VMEM((HEAD_DIM, r), BF16),
                        pltpu.VMEM((HEAD_DIM + BF16_ROWS_PER_VREG, r), F32),
                        pltpu.VMEM((1, r), F32),
                        pltpu.VMEM((4, tk, r), F32)],
        compiler_params=_params(("parallel", "parallel", "arbitrary"), est),
    )(qk3, qk3, qkv3)
    return out.reshape(batch * seq, ATT_HEADS * HEAD_DIM)


def _dn_conv_kernel(x_ref, w_ref, o_ref, xs_ref, *, rows, n_norm_blocks, q_blocks):
    seq = x_ref.shape[1]
    pad = 8
    win = rows + 2 * pad
    zeros = jnp.zeros((pad, HEAD_DIM), F32)
    xs_ref[0:pad, :] = zeros
    xs_ref[seq + pad:seq + 2 * pad, :] = zeros
    xs_ref[pad:seq + pad, :] = x_ref[0].astype(F32)
    w = w_ref[...]
    j = pl.program_id(1)
    q_scale = jnp.where(j < q_blocks, HEAD_DIM ** -0.5, 1.0).astype(F32)

    def conv_silu(r):
        base = pl.multiple_of(r * rows, rows)
        acc = jnp.zeros((rows, HEAD_DIM), F32)
        for tap in range(DN_CONV_W):
            shifted = xs_ref[pl.ds(base + (pad - DN_CONV_W // 2 + tap), rows), :]
            acc = acc + w[tap:tap + 1, :] * shifted
        return base, jax.nn.silu(acc)

    @pl.when(j < n_norm_blocks)
    def _():
        def body(r, carry):
            base, y = conv_silu(r)
            ss = jnp.sum(y * y, axis=-1, keepdims=True)
            o_ref[0, pl.ds(base, rows), :] = (y * lax.rsqrt(ss + NORM_EPS) * q_scale).astype(o_ref.dtype)
            return carry
        lax.fori_loop(0, seq // rows, body, 0)

    @pl.when(j >= n_norm_blocks)
    def _():
        def body(r, carry):
            base, y = conv_silu(r)
            o_ref[0, pl.ds(base, rows), :] = y.astype(o_ref.dtype)
            return carry
        lax.fori_loop(0, seq // rows, body, 0)


def dn_conv(proj3, conv_w):
    batch, seq, _ = proj3.shape
    width = conv_w.shape[1]
    nblk = width // HEAD_DIM
    rows = _pick(seq, 1024)
    w8 = jnp.zeros((8, width), F32).at[:DN_CONV_W].set(conv_w.astype(F32))
    est = 4 * seq * HEAD_DIM * 2 + (seq + 16) * HEAD_DIM * 4 + 16 * rows * HEAD_DIM * 4
    return pl.pallas_call(
        functools.partial(_dn_conv_kernel, rows=rows, n_norm_blocks=2 * DN_K_HEADS, q_blocks=DN_K_HEADS),
        name="dn_conv",
        grid=(batch, nblk),
        in_specs=[pl.BlockSpec((1, seq, HEAD_DIM), lambda b, j: (b, 0, j)),
                  pl.BlockSpec((8, HEAD_DIM), lambda b, j: (0, j))],
        out_specs=pl.BlockSpec((1, seq, HEAD_DIM), lambda b, j: (b, 0, j)),
        out_shape=jax.ShapeDtypeStruct((batch, seq, width), BF16),
        scratch_shapes=[pltpu.VMEM((seq + 16, HEAD_DIM), F32)],
        compiler_params=_params(("parallel", "arbitrary"), est),
    )(proj3, w8)


def _dn_gates_kernel(ba_ref, alog_ref, dtb_ref, col_ref, row_ref):
    ba = ba_ref[0]
    tg = ba.shape[0]
    lane = lax.broadcasted_iota(jnp.int32, ba.shape, 1)
    pos = lax.broadcasted_iota(jnp.int32, ba.shape, 0) & (DN_CHUNK - 1)
    beta = jax.nn.sigmoid(ba)
    g = -jnp.exp(alog_ref[...]) * jax.nn.softplus(ba + dtb_ref[...])
    g = jnp.where(lane >= 2 * DN_V_HEADS, g, 0.0)
    pre = g
    suf = g
    k = 1
    while k < DN_CHUNK:
        pre = pre + jnp.where(pos >= k, pltpu.roll(pre, k, axis=0), 0.0)
        suf = suf + jnp.where(pos < DN_CHUNK - k, pltpu.roll(suf, tg - k, axis=0), 0.0)
        k *= 2
    out = jnp.where(lane < 2 * DN_V_HEADS, beta, jnp.where(lane < 3 * DN_V_HEADS, pre, suf))
    col_ref[0] = out
    row_ref[0] = out.T


def dn_gates(ba3, a_log, dt_bias):
    batch, seq, w = ba3.shape
    tg = _pick(seq, 256)
    zeros = jnp.zeros((2 * DN_V_HEADS,), F32)
    alog_row = jnp.concatenate([zeros, a_log.astype(F32).reshape(-1)]).reshape(1, w)
    dtb_row = jnp.concatenate([zeros, dt_bias.astype(F32).reshape(-1)]).reshape(1, w)
    est = 24 * tg * w * 4
    return pl.pallas_call(
        _dn_gates_kernel,
        name="dn_gates",
        grid=(batch, seq // tg),
        in_specs=[pl.BlockSpec((1, tg, w), lambda b, i: (b, i, 0)),
                  pl.BlockSpec((1, w), lambda b, i: (0, 0)),
                  pl.BlockSpec((1, w), lambda b, i: (0, 0))],
        out_specs=[pl.BlockSpec((1, tg, w), lambda b, i: (b, i, 0)),
                   pl.BlockSpec((1, w, tg), lambda b, i: (b, 0, i))],
        out_shape=[jax.ShapeDtypeStruct((batch, seq, w), F32),
                   jax.ShapeDtypeStruct((batch, w, seq), F32)],
        compiler_params=_params(("parallel", "arbitrary"), est),
    )(ba3, alog_row, dtb_row)


def _pair_masks(backward):
    c = DN_CHUNK
    i = lax.broadcasted_iota(jnp.int32, (c, 2 * c), 0)
    lane = lax.broadcasted_iota(jnp.int32, (c, 2 * c), 1)
    j = lane & (c - 1)
    lo = lane < c
    if backward:
        i, j = j, i
    levels = []
    s = 1
    while s < c:
        sh = int(math.log2(2 * s))
        same = lax.shift_right_logical(i, sh) == lax.shift_right_logical(j, sh)
        levels.append(same & ((i & (2 * s - 1)) >= s) & ((j & (2 * s - 1)) < s))
        s *= 2
    return lo, i == j, i >= j, i > j, levels


def _block_diag2(x, lo):
    return jnp.concatenate([jnp.where(lo, x, 0.0), jnp.where(lo, 0.0, x)], axis=0).astype(BF16)


def _block_diag_wide(x):
    half = x.shape[1] // 2
    z = jnp.zeros((x.shape[0], half), x.dtype)
    return jnp.concatenate([jnp.concatenate([x[:, :half], z], axis=1),
                            jnp.concatenate([z, x[:, half:]], axis=1)], axis=0)


def _dn_setup_tasks(chain, backward, hk, kh, q_ref, k_ref, v_ref, col_ref, row_ref, o_ref):
    c = DN_CHUNK
    nt = (((1,), (1,)), ((), ()))
    n_chunks = q_ref.shape[1] // c
    dir_off = DN_V_HEADS if backward else 0
    beta_lane = dir_off + 2 * hk
    g_lane = 2 * DN_V_HEADS + dir_off + 2 * hk
    shared = {}

    def gates():
        cols = col_ref[0]
        lane = lax.broadcasted_iota(jnp.int32, cols.shape, 1)

        def column(ln):
            return jnp.sum(jnp.where(lane == ln, cols, 0.0), axis=1, keepdims=True)

        shared['beta_cols'] = (column(beta_lane), column(beta_lane + 1))
        shared['gc_cols'] = (column(g_lane), column(g_lane + 1))
        shared['gc_rows'] = (row_ref[0, pl.ds(g_lane, 1), :], row_ref[0, pl.ds(g_lane + 1, 1), :])

    def chunk(ci):
        lo, eye, incl, strict, levels = _pair_masks(backward)
        lo_row = lo[0:1]
        zero = jnp.zeros((c, HEAD_DIM), F32)
        r0 = ci * c
        v0 = (r0 // (2 * c)) * 2 * c
        row_a = shared['gc_rows'][0][:, v0:v0 + 2 * c]
        row_b = shared['gc_rows'][1][:, v0:v0 + 2 * c]
        if r0 == v0:
            grow = jnp.where(lo_row, row_a, pltpu.roll(row_b, c, axis=1))
        else:
            grow = jnp.where(lo_row, pltpu.roll(row_a, c, axis=1), row_b)
        bcol = [x[r0:r0 + c] for x in shared['beta_cols']]
        gcol = [x[r0:r0 + c] for x in shared['gc_cols']]
        g_last = [g[0:1] if backward else g[c - 1:c] for g in gcol]
        decay = jnp.exp(jnp.where(incl, jnp.where(lo, gcol[0], gcol[1]) - grow, -jnp.inf))
        kc = k_ref[0, r0:r0 + c, kh * HEAD_DIM:(kh + 1) * HEAD_DIM]
        qc = q_ref[0, r0:r0 + c, kh * HEAD_DIM:(kh + 1) * HEAD_DIM]
        kf = kc.astype(F32)
        qf = qc.astype(F32)
        vp = v_ref[0, r0:r0 + c, 2 * kh * HEAD_DIM:2 * (kh + 1) * HEAD_DIM].astype(F32)
        kb = [kf * (bcol[e] * jnp.exp(gcol[e])) for e in range(2)]
        vb = [vp[:, e * HEAD_DIM:(e + 1) * HEAD_DIM] * bcol[e] for e in range(2)]
        rhs = jnp.concatenate([jnp.concatenate([vb[0], zero, kb[0], zero], axis=1),
                               jnp.concatenate([zero, vb[1], zero, kb[1]], axis=1)], axis=0).astype(BF16)
        q_dec = jnp.concatenate([qf * jnp.exp(gcol[0]), qf * jnp.exp(gcol[1])], axis=1).astype(BF16)
        k_dec = jnp.concatenate([kf * jnp.exp(g_last[e] - gcol[e]) for e in range(2)], axis=0)
        e_last = jnp.concatenate([jnp.broadcast_to(jnp.exp(g_last[e]), (1, HEAD_DIM)) for e in range(2)], axis=1)
        kq = lax.dot_general(jnp.concatenate([kc, qc], axis=0), jnp.concatenate([kc, kc], axis=0), nt,
                             preferred_element_type=F32)
        chain.append(dict(
            r0=r0, o_ref=o_ref, o_cols=slice(2 * kh * HEAD_DIM, 2 * (kh + 1) * HEAD_DIM),
            lo=lo, eye=eye, incl=incl, strict=strict, levels=levels,
            beta=jnp.where(lo, bcol[0], bcol[1]), decay=decay, rhs=rhs, q_dec=q_dec,
            k_dec_t=k_dec.T.astype(BF16), e_last=e_last, kk=kq[:c], qk=kq[c:]))

    return [gates] + [functools.partial(chunk, ci) for ci in range(n_chunks)]


def _dn_inverse_tasks(chains, n_chunks):
    where = [(chain, ci) for chain in chains for ci in range(n_chunks)]
    n_levels = int(math.log2(DN_CHUNK))

    def init(chain, ci):
        it = chain[ci]
        lmat = jnp.where(it['strict'], it['beta'] * it['kk'] * it['decay'], 0.0)
        it['lmat'] = lmat
        it['t'] = jnp.where(it['eye'], 1.0, 0.0).astype(F32) - jnp.where(it['levels'][0], lmat, 0.0)
        it['intra'] = jnp.where(it['incl'], it['qk'] * it['decay'], 0.0).astype(BF16)

    def left(chain, ci, lv):
        it = chain[ci]
        off = _block_diag2(jnp.where(it['levels'][lv], it['lmat'], 0.0), it['lo'])
        it['x'] = jnp.dot(it['t'].astype(BF16), off, preferred_element_type=F32).astype(BF16)

    def right(chain, ci):
        it = chain[ci]
        it['t'] = it['t'] - jnp.dot(it['x'], _block_diag2(it['t'], it['lo']), preferred_element_type=F32)

    def solve(chain, ci):
        it = chain[ci]
        sol = jnp.dot(it['t'].astype(BF16), it['rhs'], preferred_element_type=F32)
        it['u'] = sol[:, :2 * HEAD_DIM]
        it['lhs_state'] = jnp.concatenate([sol[:, 2 * HEAD_DIM:].astype(BF16), it['q_dec']], axis=0)
        it['lhs_vnew'] = jnp.concatenate([it['intra'], it['k_dec_t']], axis=0)

    tasks = [functools.partial(init, *w) for w in where]
    for lv in range(1, n_levels):
        tasks += [functools.partial(left, *w, lv) for w in where]
        tasks += [functools.partial(right, *w) for w in where]
    return tasks + [functools.partial(solve, *w) for w in where]


def _dn_recurrence_tasks(group, st_ref, n_chunks):
    c = DN_CHUNK
    states, from_state, from_vnew = {}, {}, {}

    def load():
        for _, _, slot in group:
            states[slot] = st_ref[slot]

    def records(s):
        return [(chain[n_chunks - 1 - s] if backward else chain[s], slot) for chain, backward, slot in group]

    def read_state(s):
        for it, slot in records(s):
            from_state[slot] = jnp.dot(it['lhs_state'], _block_diag_wide(states[slot].astype(BF16)),
                                       preferred_element_type=F32)

    def new_values(s):
        for it, slot in records(s):
            v_new = (it['u'] - from_state[slot][:c]).astype(BF16)
            from_vnew[slot] = jnp.dot(it['lhs_vnew'], _block_diag_wide(v_new), preferred_element_type=F32)

    def emit(s):
        for it, slot in records(s):
            o = from_state[slot][c:] + from_vnew[slot][:c]
            it['o_ref'][0, it['r0']:it['r0'] + c, it['o_cols']] = o.astype(it['o_ref'].dtype)
            states[slot] = states[slot] * it['e_last'] + from_vnew[slot][c:]

    def store():
        for _, _, slot in group:
            st_ref[slot] = states[slot]

    tasks = [load]
    for s in range(n_chunks):
        tasks += [functools.partial(read_state, s), functools.partial(new_values, s), functools.partial(emit, s)]
    return tasks + [store]


def _dn_core_kernel(qf_ref, kf_ref, vf_ref, colf_ref, rowf_ref,
                    qb_ref, kb_ref, vb_ref, colb_ref, rowb_ref,
                    of_ref, ob_ref, st_ref):
    @pl.when(pl.program_id(2) == 0)
    def _():
        st_ref[...] = jnp.zeros(st_ref.shape, F32)

    nk = qf_ref.shape[2] // HEAD_DIM
    n_chunks = qf_ref.shape[1] // DN_CHUNK
    hk0 = pl.program_id(1) * nk
    fwd_refs = (qf_ref, kf_ref, vf_ref, colf_ref, rowf_ref, of_ref)
    bwd_refs = (qb_ref, kb_ref, vb_ref, colb_ref, rowb_ref, ob_ref)

    group, setup = [], []
    for backward, refs in ((False, fwd_refs), (True, bwd_refs)):
        for kh in range(nk):
            chain = []
            group.append((chain, backward, (nk if backward else 0) + kh))
            setup += _dn_setup_tasks(chain, backward, hk0 + kh, kh, *refs)
    chains = [chain for chain, _, _ in group]
    for task in setup + _dn_inverse_tasks(chains, n_chunks) + _dn_recurrence_tasks(group, st_ref, n_chunks):
        task()


def dn_core(qkv_c, gcol, grow, gt_pref=512, nk=4):
    batch, seq, _ = qkv_c.shape
    gt = _pick(seq, gt_pref)
    ng = seq // gt
    kw = nk * HEAD_DIM
    vw = 2 * kw
    n_steps = DN_K_HEADS // nk
    assert n_steps * nk == DN_K_HEADS
    k_blk0 = n_steps
    v_blk0 = n_steps
    gw = gcol.shape[2]

    def fwd(spec_cols):
        return lambda b, h, c: (b, c, spec_cols(h))

    def bwd(spec_cols):
        return lambda b, h, c: (b, ng - 1 - c, spec_cols(h))

    def specs(order):
        return [pl.BlockSpec((1, gt, kw), order(lambda h: h)),
                pl.BlockSpec((1, gt, kw), order(lambda h: k_blk0 + h)),
                pl.BlockSpec((1, gt, vw), order(lambda h: v_blk0 + h)),
                pl.BlockSpec((1, gt, gw), order(lambda h: 0)),
                pl.BlockSpec((1, gw, gt), (lambda b, h, c: (b, 0, c)) if order is fwd
                             else (lambda b, h, c: (b, 0, ng - 1 - c)))]

    out_sd = jax.ShapeDtypeStruct((batch, seq, DN_V_HEADS * HEAD_DIM), BF16)
    est = 4 * gt * (2 * kw * 2 + vw * 2 + 2 * gw * 4) + 4 * gt * vw * 2 + nk * 64 * gt * HEAD_DIM * 4
    return pl.pallas_call(
        _dn_core_kernel,
        name="dn_core",
        grid=(batch, n_steps, ng),
        in_specs=specs(fwd) + specs(bwd),
        out_specs=[pl.BlockSpec((1, gt, vw), lambda b, h, c: (b, c, h)),
                   pl.BlockSpec((1, gt, vw), lambda b, h, c: (b, ng - 1 - c, h))],
        out_shape=[out_sd, out_sd],
        scratch_shapes=[pltpu.VMEM((2 * nk, HEAD_DIM, 2 * HEAD_DIM), F32)],
        compiler_params=_params(("parallel", "parallel", "arbitrary"), est),
    )(qkv_c, qkv_c, qkv_c, gcol, grow, qkv_c, qkv_c, qkv_c, gcol, grow)


def _dn_out_kernel(of_ref, ob_ref, z_ref, g_ref, w_ref, r_ref, o_ref, a_ref):
    @pl.when(pl.program_id(1) == 0)
    def _():
        gain = g_ref[...]
        for h in range(a_ref.shape[1] // HEAD_DIM):
            sl = slice(h * HEAD_DIM, (h + 1) * HEAD_DIM)
            o = of_ref[:, sl].astype(F32) + ob_ref[:, sl].astype(F32)
            y = _rms_rows(o, gain) * jax.nn.silu(z_ref[:, sl].astype(F32))
            a_ref[:, sl] = y.astype(BF16)

    o_ref[...] = r_ref[...] + jnp.dot(a_ref[...], w_ref[...], preferred_element_type=F32)


def dn_out(o_f, o_b, proj, z_col0, out_gain, w_out, res, tm_pref=512, tn_pref=512):
    t, kdim = o_f.shape
    n = w_out.shape[1]
    tm, tn = _pick(t, tm_pref), _pick(n, tn_pref)
    z_blk = z_col0 // kdim
    assert z_col0 % kdim == 0
    est = 6 * tm * kdim * 2 + tm * kdim * 2 + 2 * kdim * tn * 2 + 4 * tm * tn * 4 + 8 * tm * HEAD_DIM * 4
    return pl.pallas_call(
        _dn_out_kernel,
        name="dn_out",
        grid=(t // tm, n // tn),
        in_specs=[pl.BlockSpec((tm, kdim), lambda i, j: (i, 0)),
                  pl.BlockSpec((tm, kdim), lambda i, j: (i, 0)),
                  pl.BlockSpec((tm, kdim), lambda i, j: (i, z_blk)),
                  pl.BlockSpec((1, HEAD_DIM), lambda i, j: (0, 0)),
                  pl.BlockSpec((kdim, tn), lambda i, j: (0, j)),
                  pl.BlockSpec((tm, tn), lambda i, j: (i, j))],
        out_specs=pl.BlockSpec((tm, tn), lambda i, j: (i, j)),
        out_shape=jax.ShapeDtypeStruct((t, n), F32),
        scratch_shapes=[pltpu.VMEM((tm, kdim), BF16)],
        compiler_params=_params(("parallel", "arbitrary"), est),
    )(o_f, o_b, proj, out_gain.reshape(1, HEAD_DIM), w_out, res)


def _deltanet_layer(h, batch, seq, norm_gain, w_in_main, w_in_ba, conv_w, a_log, dt_bias, out_gain, w_out):
    proj = norm_matmul(h, norm_gain, w_in_main, BF16)
    ba = norm_matmul(h, norm_gain, w_in_ba, F32, tn_pref=128)
    qkv_c = dn_conv(proj.reshape(batch, seq, -1), conv_w)
    gcol, grow = dn_gates(ba.reshape(batch, seq, -1), a_log, dt_bias)
    o_f, o_b = dn_core(qkv_c, gcol, grow)
    vwidth = DN_V_HEADS * HEAD_DIM
    return dn_out(o_f.reshape(batch * seq, vwidth), o_b.reshape(batch * seq, vwidth), proj,
                  conv_w.shape[1], out_gain, w_out, h)


def _attention_layer(h, batch, seq, norm_gain, w_in, qk_gains, w_out, cos2, sin2):
    qkv = norm_matmul(h, norm_gain, w_in, BF16)
    qk = qk_norm_rope(qkv, qk_gains, cos2, sin2, seq)
    o = attention(qk, qkv, batch, seq)
    return matmul_res(o, w_out, h)


def _rope_tables(n):
    rows = n // GRID_W
    row = jnp.repeat(jnp.arange(rows, dtype=F32), GRID_W)
    col = jnp.tile(jnp.arange(GRID_W, dtype=F32), rows)
    half = HEAD_DIM // 2
    inv_freq = ROPE_THETA ** (-jnp.arange(0, half, 2, dtype=F32) / half)
    ang = jnp.concatenate([row[:, None] * inv_freq, col[:, None] * inv_freq], axis=-1)
    cos, sin = jnp.cos(ang), jnp.sin(ang)
    return jnp.concatenate([cos, cos], axis=-1), jnp.concatenate([-sin, sin], axis=-1)


def _deinterleave_heads(n_heads):
    within = jnp.concatenate([jnp.arange(0, HEAD_DIM, 2), jnp.arange(1, HEAD_DIM, 2)])
    return (jnp.arange(n_heads)[:, None] * HEAD_DIM + within[None, :]).reshape(-1)


def _prepare_weights(dn_w_in, dn_w_out, at_w_in, at_q_norm, at_k_norm, at_w_out, mlp_w_up, mlp_w_down,
                     ple_w_gate, ple_w_proj):
    qkvz = dn_w_in.shape[2] - 4 * DN_V_HEADS
    n_qk = ATT_HEADS + ATT_KV_HEADS
    perm = jnp.concatenate([_deinterleave_heads(n_qk),
                            jnp.arange(n_qk * HEAD_DIM, at_w_in.shape[2])])
    within = _deinterleave_heads(1)
    scale = HEAD_DIM ** -0.5 * math.log2(math.e)
    qk_gains = jnp.concatenate(
        [jnp.broadcast_to((at_q_norm.astype(F32) * scale)[:, None, within], (at_q_norm.shape[0], ATT_HEADS, HEAD_DIM)),
         jnp.broadcast_to(at_k_norm.astype(F32)[:, None, within], (at_k_norm.shape[0], ATT_KV_HEADS, HEAD_DIM))],
        axis=1)
    return dict(
        dn_w_in_main=dn_w_in[:, :, :qkvz].astype(BF16),
        dn_w_in_ba=dn_w_in[:, :, qkvz:].astype(BF16),
        dn_w_out=dn_w_out.astype(BF16),
        at_w_in=at_w_in[:, :, perm].astype(BF16),
        qk_gains=qk_gains,
        at_w_out=at_w_out.astype(BF16),
        mlp_w_up=mlp_w_up.astype(BF16),
        mlp_w_down=mlp_w_down.astype(BF16),
        ple_w_gate=ple_w_gate.astype(BF16),
        ple_w_proj=ple_w_proj.astype(BF16),
    )


def _trunk(x, p, wts, norm_mix, norm_mlp, dn_conv_w, dn_a_log, dn_dt_bias, dn_out_norm, ple_norm):
    batch, seq, d = x.shape
    depth = p.shape[0]
    cos2, sin2 = _rope_tables(seq)
    h = x.reshape(batch * seq, d)
    for i in range(depth):
        j = i // 2
        if i % 2 == 0:
            h = _deltanet_layer(h, batch, seq, norm_mix[i], wts['dn_w_in_main'][j], wts['dn_w_in_ba'][j],
                                dn_conv_w[j], dn_a_log[j], dn_dt_bias[j], dn_out_norm[j], wts['dn_w_out'][j])
        else:
            h = _attention_layer(h, batch, seq, norm_mix[i], wts['at_w_in'][j], wts['qk_gains'][j],
                                 wts['at_w_out'][j], cos2, sin2)
        h = mlp(h, norm_mlp[i], wts['mlp_w_up'][i], wts['mlp_w_down'][i])
        h = ple(h, ple_norm[i], wts['ple_w_gate'][i], p[i].reshape(batch * seq, -1), wts['ple_w_proj'][i])
    return h.reshape(batch, seq, d)


def kernel(x_prompt, x_sample, p_prompt, p_sample, norm_mix, norm_mlp, dn_w_in, dn_conv, dn_a_log, dn_dt_bias, dn_out_norm, dn_w_out, at_w_in, at_q_norm, at_k_norm, at_w_out, mlp_w_up, mlp_w_down, ple_norm, ple_w_gate, ple_w_proj):
    wts = _prepare_weights(dn_w_in, dn_w_out, at_w_in, at_q_norm, at_k_norm, at_w_out, mlp_w_up, mlp_w_down,
                           ple_w_gate, ple_w_proj)
    shared = (wts, norm_mix, norm_mlp, dn_conv, dn_a_log, dn_dt_bias, dn_out_norm, ple_norm)
    return (_trunk(x_prompt, p_prompt, *shared), _trunk(x_sample, p_sample, *shared))
```

```python
import functools
import math

import jax
import jax.numpy as jnp
from jax import lax
from jax.experimental import pallas as pl
from jax.experimental.pallas import tpu as pltpu

F32 = jnp.float32
BF16 = jnp.bfloat16

NORM_EPS = 1e-6
GRID_W = 64
ROPE_THETA = 10000.0
HEAD_DIM = 128
DN_K_HEADS = 16
DN_V_HEADS = 32
DN_CONV_W = 5
DN_CHUNK = 64
ATT_HEADS = 16
ATT_KV_HEADS = 8

V7X_VMEM_BUDGET = 60000 * 1024
COMPILER_SCRATCH = 8 * 1024 * 1024
BF16_ROWS_PER_VREG = 16


def _pick(n, pref):
    t = pref
    while t > 8 and n % t:
        t //= 2
    assert n % t == 0, (n, pref)
    return t


def _params(semantics, est_bytes):
    limit = int(min(V7X_VMEM_BUDGET, est_bytes + COMPILER_SCRATCH))
    return pltpu.CompilerParams(dimension_semantics=semantics, vmem_limit_bytes=limit)


def _rms_rows(x, gain):
    ms = jnp.mean(x * x, axis=-1, keepdims=True)
    return x * lax.rsqrt(ms + NORM_EPS) * gain


def _norm_matmul_kernel(x_ref, g_ref, w_ref, o_ref, xn_ref):
    @pl.when(pl.program_id(1) == 0)
    def _():
        xn_ref[...] = _rms_rows(x_ref[...], g_ref[...]).astype(BF16)

    o_ref[...] = jnp.dot(xn_ref[...], w_ref[...], preferred_element_type=F32).astype(o_ref.dtype)


def norm_matmul(x, gain, w, out_dtype, tm_pref=1024, tn_pref=1024):
    t, d = x.shape
    n = w.shape[1]
    tm, tn = _pick(t, tm_pref), _pick(n, tn_pref)
    osz = jnp.dtype(out_dtype).itemsize
    est = 2 * tm * d * 4 + tm * d * 2 + 2 * d * tn * 2 + 2 * tm * tn * osz + tm * tn * 4
    return pl.pallas_call(
        _norm_matmul_kernel,
        name="norm_matmul",
        grid=(t // tm, n // tn),
        in_specs=[pl.BlockSpec((tm, d), lambda i, j: (i, 0)),
                  pl.BlockSpec((1, d), lambda i, j: (0, 0)),
                  pl.BlockSpec((d, tn), lambda i, j: (0, j))],
        out_specs=pl.BlockSpec((tm, tn), lambda i, j: (i, j)),
        out_shape=jax.ShapeDtypeStruct((t, n), out_dtype),
        scratch_shapes=[pltpu.VMEM((tm, d), BF16)],
        compiler_params=_params(("parallel", "arbitrary"), est),
    )(x, gain.reshape(1, d), w)


def _matmul_res_kernel(a_ref, w_ref, r_ref, o_ref):
    o_ref[...] = r_ref[...] + jnp.dot(a_ref[...], w_ref[...], preferred_element_type=F32)


def matmul_res(a, w, res, tm_pref=1024, tn_pref=1024):
    t, k = a.shape
    n = w.shape[1]
    tm, tn = _pick(t, tm_pref), _pick(n, tn_pref)
    est = 2 * tm * k * 2 + 2 * k * tn * 2 + 4 * tm * tn * 4 + tm * tn * 4
    return pl.pallas_call(
        _matmul_res_kernel,
        name="matmul_res",
        grid=(t // tm, n // tn),
        in_specs=[pl.BlockSpec((tm, k), lambda i, j: (i, 0)),
                  pl.BlockSpec((k, tn), lambda i, j: (0, j)),
                  pl.BlockSpec((tm, tn), lambda i, j: (i, j))],
        out_specs=pl.BlockSpec((tm, tn), lambda i, j: (i, j)),
        out_shape=jax.ShapeDtypeStruct((t, n), F32),
        compiler_params=_params(("parallel", "arbitrary"), est),
    )(a, w, res)


def _mlp_kernel(x_ref, g_ref, wu_ref, wd_ref, o_ref, xn_ref):
    @pl.when(pl.program_id(1) == 0)
    def _():
        x = x_ref[...]
        xn_ref[...] = _rms_rows(x, g_ref[...]).astype(BF16)
        o_ref[...] = x

    hid = jnp.maximum(jnp.dot(xn_ref[...], wu_ref[...], preferred_element_type=F32), 0.0)
    hid = (hid * hid).astype(BF16)
    o_ref[...] += jnp.dot(hid, wd_ref[...], preferred_element_type=F32)


def mlp(x, gain, w_up, w_down, tm_pref=512, tf_pref=1024):
    t, d = x.shape
    f = w_up.shape[1]
    tm, tf = _pick(t, tm_pref), _pick(f, tf_pref)
    est = 4 * tm * d * 4 + tm * d * 2 + 4 * d * tf * 2 + tm * tf * 6 + tm * d * 4
    return pl.pallas_call(
        _mlp_kernel,
        name="mlp",
        grid=(t // tm, f // tf),
        in_specs=[pl.BlockSpec((tm, d), lambda i, j: (i, 0)),
                  pl.BlockSpec((1, d), lambda i, j: (0, 0)),
                  pl.BlockSpec((d, tf), lambda i, j: (0, j)),
                  pl.BlockSpec((tf, d), lambda i, j: (j, 0))],
        out_specs=pl.BlockSpec((tm, d), lambda i, j: (i, 0)),
        out_shape=jax.ShapeDtypeStruct((t, d), F32),
        scratch_shapes=[pltpu.VMEM((tm, d), BF16)],
        compiler_params=_params(("parallel", "arbitrary"), est),
    )(x, gain.reshape(1, d), w_up, w_down)


def _ple_kernel(x_ref, g_ref, wg_ref, p_ref, wp_ref, o_ref, *, sub):
    for s0 in range(0, x_ref.shape[0], sub):
        rows = slice(s0, s0 + sub)
        x = x_ref[rows, :]
        xn = _rms_rows(x, g_ref[...]).astype(BF16)
        gate = jax.nn.sigmoid(jnp.dot(xn, wg_ref[...], preferred_element_type=F32))
        emb = jnp.dot(p_ref[rows, :].astype(BF16), wp_ref[...], preferred_element_type=F32)
        o_ref[rows, :] = x + gate * emb


def ple(x, gain, w_gate, p, w_proj, tm_pref=512, sub_pref=256):
    t, d = x.shape
    pd = p.shape[1]
    tm = _pick(t, tm_pref)
    sub = _pick(tm, sub_pref)
    est = 4 * tm * d * 4 + 2 * d * d * 2 + 2 * tm * pd * 4 + 2 * pd * d * 2 + 6 * sub * d * 4
    return pl.pallas_call(
        functools.partial(_ple_kernel, sub=sub),
        name="ple",
        grid=(t // tm,),
        in_specs=[pl.BlockSpec((tm, d), lambda i: (i, 0)),
                  pl.BlockSpec((1, d), lambda i: (0, 0)),
                  pl.BlockSpec((d, d), lambda i: (0, 0)),
                  pl.BlockSpec((tm, pd), lambda i: (i, 0)),
                  pl.BlockSpec((pd, d), lambda i: (0, 0))],
        out_specs=pl.BlockSpec((tm, d), lambda i: (i, 0)),
        out_shape=jax.ShapeDtypeStruct((t, d), F32),
        compiler_params=_params(("parallel",), est),
    )(x, gain.reshape(1, d), w_gate, p, w_proj)


def _qk_rope_kernel(x_ref, g_ref, c_ref, s_ref, o_ref):
    cos, sin = c_ref[...], s_ref[...]
    for h in range(g_ref.shape[0]):
        sl = slice(h * HEAD_DIM, (h + 1) * HEAD_DIM)
        y = _rms_rows(x_ref[:, sl].astype(F32), g_ref[h])
        o_ref[:, sl] = (y * cos + pltpu.roll(y, HEAD_DIM // 2, axis=1) * sin).astype(o_ref.dtype)


def qk_norm_rope(qkv, gains, cos2, sin2, seq, hb_pref=8):
    t = qkv.shape[0]
    nh = gains.shape[0]
    hb = _pick(nh, hb_pref)
    tm = _pick(seq, 512)
    per_seq = seq // tm
    est = 4 * tm * hb * HEAD_DIM * 2 + 4 * tm * HEAD_DIM * 4 + 8 * tm * HEAD_DIM * 4
    return pl.pallas_call(
        _qk_rope_kernel,
        name="qk_rope",
        grid=(t // tm, nh // hb),
        in_specs=[pl.BlockSpec((tm, hb * HEAD_DIM), lambda i, j: (i, j)),
                  pl.BlockSpec((hb, 1, HEAD_DIM), lambda i, j: (j, 0, 0)),
                  pl.BlockSpec((tm, HEAD_DIM), lambda i, j: (i % per_seq, 0)),
                  pl.BlockSpec((tm, HEAD_DIM), lambda i, j: (i % per_seq, 0))],
        out_specs=pl.BlockSpec((tm, hb * HEAD_DIM), lambda i, j: (i, j)),
        out_shape=jax.ShapeDtypeStruct((t, nh * HEAD_DIM), BF16),
        compiler_params=_params(("parallel", "arbitrary"), est),
    )(qkv, gains.reshape(nh, 1, HEAD_DIM), cos2, sin2)


PAIRS_PER_TRIP = 4


def _attn_kernel(q_ref, k_ref, v_ref, o_ref, vt_ref, qt_ref, acc_ref, m_ref, s_ref, *, tk, cb):
    tq = q_ref.shape[1]
    seq = k_ref.shape[1]
    r = qt_ref.shape[1]

    @pl.when(pl.program_id(2) == 0)
    def _():
        ones = jnp.ones((vt_ref.shape[1] - HEAD_DIM, tk), BF16)
        for c in range(seq // tk):
            vt_ref[c, :HEAD_DIM, :] = v_ref[0, c * tk:(c + 1) * tk, :].astype(F32).T.astype(BF16)
            vt_ref[c, HEAD_DIM:, :] = ones

    q = q_ref[0]
    q2 = jnp.concatenate([q[:, :HEAD_DIM], q[:, HEAD_DIM:]], axis=0)
    qt_ref[...] = q2.astype(F32).T.astype(BF16)
    m_ref[...] = jnp.full(m_ref.shape, -jnp.inf, F32)
    acc_ref[...] = jnp.zeros(acc_ref.shape, F32)

    blocks = [slice(b * cb, (b + 1) * cb) for b in range(r // cb)]
    n_chunks = seq // tk

    def scores(c, slot):
        start = c * tk if isinstance(c, int) else pl.multiple_of(c * tk, tk)
        kc = k_ref[0, pl.ds(start, tk), :]
        for sl in blocks:
            s_ref[slot, :, sl] = jnp.dot(kc, qt_ref[:, sl], preferred_element_type=F32)

    def softmax_pv(c, slot):
        vtc = vt_ref[c]
        s = [s_ref[slot, :, sl] for sl in blocks]
        m_old = [m_ref[:, sl] for sl in blocks]
        m_new = [jnp.maximum(mo, jnp.max(sb, axis=0, keepdims=True)) for mo, sb in zip(m_old, s)]
        alpha = [jnp.exp2(mo - mn) for mo, mn in zip(m_old, m_new)]
        p = [jnp.exp2(sb - mn) for sb, mn in zip(s, m_new)]
        for sl, mn in zip(blocks, m_new):
            m_ref[:, sl] = mn
        return alpha, [jnp.dot(vtc, pb.astype(BF16), preferred_element_type=F32) for pb in p]

    def accumulate(alpha, pv):
        for sl, a, pvb in zip(blocks, alpha, pv):
            acc_ref[:, sl] = a * acc_ref[:, sl] + pvb

    def pair(c2, cur, with_next):
        c = 2 * c2
        nxt = 2 - cur
        if with_next:
            scores(c + 2, nxt)
        part_a = softmax_pv(c, cur)
        if with_next:
            scores(c + 3, nxt + 1)
        part_b = softmax_pv(c + 1, cur + 1)
        accumulate(*part_a)
        accumulate(*part_b)

    def trip(i, last):
        for j in range(PAIRS_PER_TRIP):
            pair(PAIRS_PER_TRIP * i + j, 2 * (j % 2), not (last and j == PAIRS_PER_TRIP - 1))

    n_trips = n_chunks // (2 * PAIRS_PER_TRIP)
    scores(0, 0)
    scores(1, 1)

    def body(i, carry):
        trip(i, False)
        return carry

    lax.fori_loop(0, n_trips - 1, body, 0)
    trip(n_trips - 1, True)
    o = (acc_ref[:HEAD_DIM, :] / acc_ref[HEAD_DIM:HEAD_DIM + 1, :]).T
    o_ref[0] = jnp.concatenate([o[:tq], o[tq:]], axis=1).astype(o_ref.dtype)


def attention(qk, qkv, batch, seq, tq_pref=2048, tk_pref=256, cb_pref=256):
    qk3 = qk.reshape(batch, seq, qk.shape[1])
    qkv3 = qkv.reshape(batch, seq, qkv.shape[1])
    group = ATT_HEADS // ATT_KV_HEADS
    qw = group * HEAD_DIM
    tq, tk = _pick(seq, tq_pref), _pick(seq // (2 * PAIRS_PER_TRIP), tk_pref)
    assert (seq // tk) % (2 * PAIRS_PER_TRIP) == 0
    r = group * tq
    cb = _pick(r, cb_pref)
    k_blk0 = ATT_HEADS
    v_blk0 = ATT_HEADS + ATT_KV_HEADS
    est = (4 * tq * qw * 2 + 4 * seq * HEAD_DIM * 2 + seq * HEAD_DIM * 2 + r * HEAD_DIM * 6
           + 8 * tk * cb * 4 + 4 * r * HEAD_DIM * 4 + 4 * tk * r * 4)
    out = pl.pallas_call(
        functools.partial(_attn_kernel, tk=tk, cb=cb),
        name="attention",
        grid=(batch, ATT_KV_HEADS, seq // tq),
        in_specs=[pl.BlockSpec((1, tq, qw), lambda b, h, i: (b, i, h)),
                  pl.BlockSpec((1, seq, HEAD_DIM), lambda b, h, i: (b, 0, k_blk0 + h)),
                  pl.BlockSpec((1, seq, HEAD_DIM), lambda b, h, i: (b, 0, v_blk0 + h))],
        out_specs=pl.BlockSpec((1, tq, qw), lambda b, h, i: (b, i, h)),
        out_shape=jax.ShapeDtypeStruct((batch, seq, ATT_HEADS * HEAD_DIM), BF16),
        scratch_shapes=[pltpu.VMEM((seq // tk, HEAD_DIM + BF16_ROWS_PER_VREG, tk), BF16),
                        pltpu.VMEM((HEAD_DIM, r), BF16),
                        pltpu.VMEM((HEAD_DIM + BF16_ROWS_PER_VREG, r), F32),
                        pltpu.VMEM((1, r), F32),
                        pltpu.VMEM((4, tk, r), F32)],
        compiler_params=_params(("parallel", "parallel", "arbitrary"), est),
    )(qk3, qk3, qkv3)
    return out.reshape(batch * seq, ATT_HEADS * HEAD_DIM)


def _dn_conv_kernel(x_ref, w_ref, o_ref, xs_ref, *, rows, n_norm_blocks, q_blocks):
    seq = x_ref.shape[1]
    pad = 8
    win = rows + 2 * pad
    zeros = jnp.zeros((pad, HEAD_DIM), F32)
    xs_ref[0:pad, :] = zeros
    xs_ref[seq + pad:seq + 2 * pad, :] = zeros
    xs_ref[pad:seq + pad, :] = x_ref[0].astype(F32)
    w = w_ref[...]
    j = pl.program_id(1)
    q_scale = jnp.where(j < q_blocks, HEAD_DIM ** -0.5, 1.0).astype(F32)

    def conv_silu(r):
        base = pl.multiple_of(r * rows, rows)
        acc = jnp.zeros((rows, HEAD_DIM), F32)
        for tap in range(DN_CONV_W):
            shifted = xs_ref[pl.ds(base + (pad - DN_CONV_W // 2 + tap), rows), :]
            acc = acc + w[tap:tap + 1, :] * shifted
        return base, jax.nn.silu(acc)

    @pl.when(j < n_norm_blocks)
    def _():
        def body(r, carry):
            base, y = conv_silu(r)
            ss = jnp.sum(y * y, axis=-1, keepdims=True)
            o_ref[0, pl.ds(base, rows), :] = (y * lax.rsqrt(ss + NORM_EPS) * q_scale).astype(o_ref.dtype)
            return carry
        lax.fori_loop(0, seq // rows, body, 0)

    @pl.when(j >= n_norm_blocks)
    def _():
        def body(r, carry):
            base, y = conv_silu(r)
            o_ref[0, pl.ds(base, rows), :] = y.astype(o_ref.dtype)
            return carry
        lax.fori_loop(0, seq // rows, body, 0)


def dn_conv(proj3, conv_w):
    batch, seq, _ = proj3.shape
    width = conv_w.shape[1]
    nblk = width // HEAD_DIM
    rows = _pick(seq, 1024)
    w8 = jnp.zeros((8, width), F32).at[:DN_CONV_W].set(conv_w.astype(F32))
    est = 4 * seq * HEAD_DIM * 2 + (seq + 16) * HEAD_DIM * 4 + 16 * rows * HEAD_DIM * 4
    return pl.pallas_call(
        functools.partial(_dn_conv_kernel, rows=rows, n_norm_blocks=2 * DN_K_HEADS, q_blocks=DN_K_HEADS),
        name="dn_conv",
        grid=(batch, nblk),
        in_specs=[pl.BlockSpec((1, seq, HEAD_DIM), lambda b, j: (b, 0, j)),
                  pl.BlockSpec((8, HEAD_DIM), lambda b, j: (0, j))],
        out_specs=pl.BlockSpec((1, seq, HEAD_DIM), lambda b, j: (b, 0, j)),
        out_shape=jax.ShapeDtypeStruct((batch, seq, width), BF16),
        scratch_shapes=[pltpu.VMEM((seq + 16, HEAD_DIM), F32)],
        compiler_params=_params(("parallel", "arbitrary"), est),
    )(proj3, w8)


def _dn_gates_kernel(ba_ref, alog_ref, dtb_ref, col_ref, row_ref):
    ba = ba_ref[0]
    tg = ba.shape[0]
    lane = lax.broadcasted_iota(jnp.int32, ba.shape, 1)
    pos = lax.broadcasted_iota(jnp.int32, ba.shape, 0) & (DN_CHUNK - 1)
    beta = jax.nn.sigmoid(ba)
    g = -jnp.exp(alog_ref[...]) * jax.nn.softplus(ba + dtb_ref[...])
    g = jnp.where(lane >= 2 * DN_V_HEADS, g, 0.0)
    pre = g
    suf = g
    k = 1
    while k < DN_CHUNK:
        pre = pre + jnp.where(pos >= k, pltpu.roll(pre, k, axis=0), 0.0)
        suf = suf + jnp.where(pos < DN_CHUNK - k, pltpu.roll(suf, tg - k, axis=0), 0.0)
        k *= 2
    out = jnp.where(lane < 2 * DN_V_HEADS, beta, jnp.where(lane < 3 * DN_V_HEADS, pre, suf))
    col_ref[0] = out
    row_ref[0] = out.T


def dn_gates(ba3, a_log, dt_bias):
    batch, seq, w = ba3.shape
    tg = _pick(seq, 256)
    zeros = jnp.zeros((2 * DN_V_HEADS,), F32)
    alog_row = jnp.concatenate([zeros, a_log.astype(F32).reshape(-1)]).reshape(1, w)
    dtb_row = jnp.concatenate([zeros, dt_bias.astype(F32).reshape(-1)]).reshape(1, w)
    est = 24 * tg * w * 4
    return pl.pallas_call(
        _dn_gates_kernel,
        name="dn_gates",
        grid=(batch, seq // tg),
        in_specs=[pl.BlockSpec((1, tg, w), lambda b, i: (b, i, 0)),
                  pl.BlockSpec((1, w), lambda b, i: (0, 0)),
                  pl.BlockSpec((1, w), lambda b, i: (0, 0))],
        out_specs=[pl.BlockSpec((1, tg, w), lambda b, i: (b, i, 0)),
                   pl.BlockSpec((1, w, tg), lambda b, i: (b, 0, i))],
        out_shape=[jax.ShapeDtypeStruct((batch, seq, w), F32),
                   jax.ShapeDtypeStruct((batch, w, seq), F32)],
        compiler_params=_params(("parallel", "arbitrary"), est),
    )(ba3, alog_row, dtb_row)


def _pair_masks(backward):
    c = DN_CHUNK
    i = lax.broadcasted_iota(jnp.int32, (c, 2 * c), 0)
    lane = lax.broadcasted_iota(jnp.int32, (c, 2 * c), 1)
    j = lane & (c - 1)
    lo = lane < c
    if backward:
        i, j = j, i
    levels = []
    s = 1
    while s < c:
        sh = int(math.log2(2 * s))
        same = lax.shift_right_logical(i, sh) == lax.shift_right_logical(j, sh)
        levels.append(same & ((i & (2 * s - 1)) >= s) & ((j & (2 * s - 1)) < s))
        s *= 2
    return lo, i == j, i >= j, i > j, levels


def _block_diag2(x, lo):
    return jnp.concatenate([jnp.where(lo, x, 0.0), jnp.where(lo, 0.0, x)], axis=0).astype(BF16)


def _block_diag_wide(x):
    half = x.shape[1] // 2
    z = jnp.zeros((x.shape[0], half), x.dtype)
    return jnp.concatenate([jnp.concatenate([x[:, :half], z], axis=1),
                            jnp.concatenate([z, x[:, half:]], axis=1)], axis=0)


def _dn_setup_tasks(chain, chain_ops, backward, hk, kh, q_ref, k_ref, v_ref, col_ref, row_ref, o_ref):
    c = DN_CHUNK
    nt = (((1,), (1,)), ((), ()))
    n_chunks = q_ref.shape[1] // c
    dir_off = DN_V_HEADS if backward else 0
    beta_lane = dir_off + 2 * hk
    g_lane = 2 * DN_V_HEADS + dir_off + 2 * hk
    shared = {}

    def gates():
        cols = col_ref[0]
        lane = lax.broadcasted_iota(jnp.int32, cols.shape, 1)

        def column(ln):
            return jnp.sum(jnp.where(lane == ln, cols, 0.0), axis=1, keepdims=True)

        shared['beta_cols'] = (column(beta_lane), column(beta_lane + 1))
        shared['gc_cols'] = (column(g_lane), column(g_lane + 1))
        shared['gc_rows'] = (row_ref[0, pl.ds(g_lane, 1), :], row_ref[0, pl.ds(g_lane + 1, 1), :])

    def chunk(ci):
        lo, eye, incl, strict, levels = _pair_masks(backward)
        lo_row = lo[0:1]
        r0 = ci * c
        v0 = (r0 // (2 * c)) * 2 * c
        row_a = shared['gc_rows'][0][:, v0:v0 + 2 * c]
        row_b = shared['gc_rows'][1][:, v0:v0 + 2 * c]
        if r0 == v0:
            grow = jnp.where(lo_row, row_a, pltpu.roll(row_b, c, axis=1))
        else:
            grow = jnp.where(lo_row, pltpu.roll(row_a, c, axis=1), row_b)
        bcol = [x[r0:r0 + c] for x in shared['beta_cols']]
        gcol = [x[r0:r0 + c] for x in shared['gc_cols']]
        decay = jnp.exp(jnp.where(incl, jnp.where(lo, gcol[0], gcol[1]) - grow, -jnp.inf))
        kc = k_ref[0, r0:r0 + c, kh * HEAD_DIM:(kh + 1) * HEAD_DIM]
        qc = q_ref[0, r0:r0 + c, kh * HEAD_DIM:(kh + 1) * HEAD_DIM]
        kq = lax.dot_general(jnp.concatenate([kc, qc], axis=0), jnp.concatenate([kc, kc], axis=0), nt,
                             preferred_element_type=F32)
        chain.append(dict(
            r0=r0, o_ref=o_ref, o_cols=slice(2 * kh * HEAD_DIM, 2 * (kh + 1) * HEAD_DIM),
            lo=lo, eye=eye, incl=incl, strict=strict, levels=levels, bcol=bcol, gcol=gcol,
            beta=jnp.where(lo, bcol[0], bcol[1]), decay=decay, kk=kq[:c], qk=kq[c:]))

    def operands(it):
        r0, bcol, gcol = it['r0'], it['bcol'], it['gcol']
        zero = jnp.zeros((c, HEAD_DIM), F32)
        g_last = [g[0:1] if backward else g[c - 1:c] for g in gcol]
        kf = k_ref[0, r0:r0 + c, kh * HEAD_DIM:(kh + 1) * HEAD_DIM].astype(F32)
        qf = q_ref[0, r0:r0 + c, kh * HEAD_DIM:(kh + 1) * HEAD_DIM].astype(F32)
        vp = v_ref[0, r0:r0 + c, 2 * kh * HEAD_DIM:2 * (kh + 1) * HEAD_DIM].astype(F32)
        kb = [kf * (bcol[e] * jnp.exp(gcol[e])) for e in range(2)]
        vb = [vp[:, e * HEAD_DIM:(e + 1) * HEAD_DIM] * bcol[e] for e in range(2)]
        rhs = jnp.concatenate([jnp.concatenate([vb[0], zero, kb[0], zero], axis=1),
                               jnp.concatenate([zero, vb[1], zero, kb[1]], axis=1)], axis=0).astype(BF16)
        q_dec = jnp.concatenate([qf * jnp.exp(gcol[0]), qf * jnp.exp(gcol[1])], axis=1).astype(BF16)
        k_dec = jnp.concatenate([kf * jnp.exp(g_last[e] - gcol[e]) for e in range(2)], axis=0)
        e_last = jnp.concatenate([jnp.broadcast_to(jnp.exp(g_last[e]), (1, HEAD_DIM)) for e in range(2)], axis=1)
        return rhs, q_dec, k_dec.T.astype(BF16), e_last

    chain_ops.append(operands)

    return [gates] + [functools.partial(chunk, ci) for ci in range(n_chunks)]


def _dn_inverse_tasks(chains, n_chunks):
    where = [(chain, ops, ci) for chain, ops in chains for ci in range(n_chunks)]
    n_levels = int(math.log2(DN_CHUNK))

    def init(chain, ops, ci):
        it = chain[ci]
        lmat = jnp.where(it['strict'], it['beta'] * it['kk'] * it['decay'], 0.0)
        it['lmat'] = lmat
        it['t'] = jnp.where(it['eye'], 1.0, 0.0).astype(F32) - jnp.where(it['levels'][0], lmat, 0.0)
        it['intra'] = jnp.where(it['incl'], it['qk'] * it['decay'], 0.0).astype(BF16)

    def left(chain, ops, ci, lv):
        it = chain[ci]
        off = _block_diag2(jnp.where(it['levels'][lv], it['lmat'], 0.0), it['lo'])
        it['x'] = jnp.dot(it['t'].astype(BF16), off, preferred_element_type=F32).astype(BF16)

    def right(chain, ops, ci):
        it = chain[ci]
        it['t'] = it['t'] - jnp.dot(it['x'], _block_diag2(it['t'], it['lo']), preferred_element_type=F32)

    def solve(chain, ops, ci):
        it = chain[ci]
        rhs, q_dec, k_dec_t, it['e_last'] = ops[0](it)
        sol = jnp.dot(it['t'].astype(BF16), rhs, preferred_element_type=F32)
        it['u'] = sol[:, :2 * HEAD_DIM]
        it['lhs_state'] = jnp.concatenate([sol[:, 2 * HEAD_DIM:].astype(BF16), q_dec], axis=0)
        it['lhs_vnew'] = jnp.concatenate([it['intra'], k_dec_t], axis=0)

    tasks = [functools.partial(init, *w) for w in where]
    for lv in range(1, n_levels):
        tasks += [functools.partial(left, *w, lv) for w in where]
        tasks += [functools.partial(right, *w) for w in where]
    return tasks + [functools.partial(solve, *w) for w in where]


def _dn_recurrence_tasks(group, st_ref, n_chunks):
    c = DN_CHUNK
    states, from_state, from_vnew = {}, {}, {}

    def load():
        for _, _, slot in group:
            states[slot] = st_ref[slot]

    def records(s):
        return [(chain[n_chunks - 1 - s] if backward else chain[s], slot) for chain, backward, slot in group]

    def read_state(s):
        for it, slot in records(s):
            from_state[slot] = jnp.dot(it['lhs_state'], _block_diag_wide(states[slot].astype(BF16)),
                                       preferred_element_type=F32)

    def new_values(s):
        for it, slot in records(s):
            v_new = (it['u'] - from_state[slot][:c]).astype(BF16)
            from_vnew[slot] = jnp.dot(it['lhs_vnew'], _block_diag_wide(v_new), preferred_element_type=F32)

    def emit(s):
        for it, slot in records(s):
            o = from_state[slot][c:] + from_vnew[slot][:c]
            it['o_ref'][0, it['r0']:it['r0'] + c, it['o_cols']] = o.astype(it['o_ref'].dtype)
            states[slot] = states[slot] * it['e_last'] + from_vnew[slot][c:]

    def store():
        for _, _, slot in group:
            st_ref[slot] = states[slot]

    tasks = [load]
    for s in range(n_chunks):
        tasks += [functools.partial(read_state, s), functools.partial(new_values, s), functools.partial(emit, s)]
    return tasks + [store]


def _dn_core_kernel(qf_ref, kf_ref, vf_ref, colf_ref, rowf_ref,
                    qb_ref, kb_ref, vb_ref, colb_ref, rowb_ref,
                    of_ref, ob_ref, st_ref):
    @pl.when(pl.program_id(2) == 0)
    def _():
        st_ref[...] = jnp.zeros(st_ref.shape, F32)

    nk = qf_ref.shape[2] // HEAD_DIM
    n_chunks = qf_ref.shape[1] // DN_CHUNK
    hk0 = pl.program_id(1) * nk
    fwd_refs = (qf_ref, kf_ref, vf_ref, colf_ref, rowf_ref, of_ref)
    bwd_refs = (qb_ref, kb_ref, vb_ref, colb_ref, rowb_ref, ob_ref)

    group, chains, setup = [], [], []
    for backward, refs in ((False, fwd_refs), (True, bwd_refs)):
        for kh in range(nk):
            chain, chain_ops = [], []
            group.append((chain, backward, (nk if backward else 0) + kh))
            chains.append((chain, chain_ops))
            setup += _dn_setup_tasks(chain, chain_ops, backward, hk0 + kh, kh, *refs)
    for task in setup + _dn_inverse_tasks(chains, n_chunks) + _dn_recurrence_tasks(group, st_ref, n_chunks):
        task()


def dn_core(qkv_c, gcol, grow, gt_pref=512, nk=4):
    batch, seq, _ = qkv_c.shape
    gt = _pick(seq, gt_pref)
    ng = seq // gt
    kw = nk * HEAD_DIM
    vw = 2 * kw
    n_steps = DN_K_HEADS // nk
    assert n_steps * nk == DN_K_HEADS
    k_blk0 = n_steps
    v_blk0 = n_steps
    gw = gcol.shape[2]

    def fwd(spec_cols):
        return lambda b, h, c: (b, c, spec_cols(h))

    def bwd(spec_cols):
        return lambda b, h, c: (b, ng - 1 - c, spec_cols(h))

    def specs(order):
        return [pl.BlockSpec((1, gt, kw), order(lambda h: h)),
                pl.BlockSpec((1, gt, kw), order(lambda h: k_blk0 + h)),
                pl.BlockSpec((1, gt, vw), order(lambda h: v_blk0 + h)),
                pl.BlockSpec((1, gt, gw), order(lambda h: 0)),
                pl.BlockSpec((1, gw, gt), (lambda b, h, c: (b, 0, c)) if order is fwd
                             else (lambda b, h, c: (b, 0, ng - 1 - c)))]

    out_sd = jax.ShapeDtypeStruct((batch, seq, DN_V_HEADS * HEAD_DIM), BF16)
    est = 4 * gt * (2 * kw * 2 + vw * 2 + 2 * gw * 4) + 4 * gt * vw * 2 + nk * 64 * gt * HEAD_DIM * 4
    return pl.pallas_call(
        _dn_core_kernel,
        name="dn_core",
        grid=(batch, n_steps, ng),
        in_specs=specs(fwd) + specs(bwd),
        out_specs=[pl.BlockSpec((1, gt, vw), lambda b, h, c: (b, c, h)),
                   pl.BlockSpec((1, gt, vw), lambda b, h, c: (b, ng - 1 - c, h))],
        out_shape=[out_sd, out_sd],
        scratch_shapes=[pltpu.VMEM((2 * nk, HEAD_DIM, 2 * HEAD_DIM), F32)],
        compiler_params=_params(("parallel", "parallel", "arbitrary"), est),
    )(qkv_c, qkv_c, qkv_c, gcol, grow, qkv_c, qkv_c, qkv_c, gcol, grow)


def _dn_gate_kernel(of_ref, ob_ref, z_ref, g_ref, a_ref):
    gain = g_ref[...]
    for h in range(a_ref.shape[1] // HEAD_DIM):
        sl = slice(h * HEAD_DIM, (h + 1) * HEAD_DIM)
        o = of_ref[:, sl].astype(F32) + ob_ref[:, sl].astype(F32)
        y = _rms_rows(o, gain) * jax.nn.silu(z_ref[:, sl].astype(F32))
        a_ref[:, sl] = y.astype(BF16)


def dn_out(o_f, o_b, proj, z_col0, out_gain, w_out, res, tm_pref=512, hb_pref=8):
    t, kdim = o_f.shape
    tm = _pick(t, tm_pref)
    wblk = _pick(kdim // HEAD_DIM, hb_pref) * HEAD_DIM
    z_blk0 = z_col0 // wblk
    assert z_col0 % wblk == 0
    est = 8 * tm * wblk * 2 + 10 * tm * HEAD_DIM * 4
    gated = pl.pallas_call(
        _dn_gate_kernel,
        name="dn_gate",
        grid=(t // tm, kdim // wblk),
        in_specs=[pl.BlockSpec((tm, wblk), lambda i, j: (i, j)),
                  pl.BlockSpec((tm, wblk), lambda i, j: (i, j)),
                  pl.BlockSpec((tm, wblk), lambda i, j: (i, z_blk0 + j)),
                  pl.BlockSpec((1, HEAD_DIM), lambda i, j: (0, 0))],
        out_specs=pl.BlockSpec((tm, wblk), lambda i, j: (i, j)),
        out_shape=jax.ShapeDtypeStruct((t, kdim), BF16),
        compiler_params=_params(("parallel", "arbitrary"), est),
    )(o_f, o_b, proj, out_gain.reshape(1, HEAD_DIM))
    return matmul_res(gated, w_out, res, tm_pref=512)


def _deltanet_layer(h, batch, seq, norm_gain, w_in_main, w_in_ba, conv_w, a_log, dt_bias, out_gain, w_out):
    proj = norm_matmul(h, norm_gain, w_in_main, BF16)
    ba = norm_matmul(h, norm_gain, w_in_ba, F32, tn_pref=128)
    qkv_c = dn_conv(proj.reshape(batch, seq, -1), conv_w)
    gcol, grow = dn_gates(ba.reshape(batch, seq, -1), a_log, dt_bias)
    o_f, o_b = dn_core(qkv_c, gcol, grow)
    vwidth = DN_V_HEADS * HEAD_DIM
    return dn_out(o_f.reshape(batch * seq, vwidth), o_b.reshape(batch * seq, vwidth), proj,
                  conv_w.shape[1], out_gain, w_out, h)


def _attention_layer(h, batch, seq, norm_gain, w_in, qk_gains, w_out, cos2, sin2):
    qkv = norm_matmul(h, norm_gain, w_in, BF16)
    qk = qk_norm_rope(qkv, qk_gains, cos2, sin2, seq)
    o = attention(qk, qkv, batch, seq)
    return matmul_res(o, w_out, h)


def _rope_tables(n):
    rows = n // GRID_W
    row = jnp.repeat(jnp.arange(rows, dtype=F32), GRID_W)
    col = jnp.tile(jnp.arange(GRID_W, dtype=F32), rows)
    half = HEAD_DIM // 2
    inv_freq = ROPE_THETA ** (-jnp.arange(0, half, 2, dtype=F32) / half)
    ang = jnp.concatenate([row[:, None] * inv_freq, col[:, None] * inv_freq], axis=-1)
    cos, sin = jnp.cos(ang), jnp.sin(ang)
    return jnp.concatenate([cos, cos], axis=-1), jnp.concatenate([-sin, sin], axis=-1)


def _deinterleave_heads(n_heads):
    within = jnp.concatenate([jnp.arange(0, HEAD_DIM, 2), jnp.arange(1, HEAD_DIM, 2)])
    return (jnp.arange(n_heads)[:, None] * HEAD_DIM + within[None, :]).reshape(-1)


def _prepare_weights(dn_w_in, dn_w_out, at_w_in, at_q_norm, at_k_norm, at_w_out, mlp_w_up, mlp_w_down,
                     ple_w_gate, ple_w_proj):
    qkvz = dn_w_in.shape[2] - 4 * DN_V_HEADS
    n_qk = ATT_HEADS + ATT_KV_HEADS
    perm = jnp.concatenate([_deinterleave_heads(n_qk),
                            jnp.arange(n_qk * HEAD_DIM, at_w_in.shape[2])])
    within = _deinterleave_heads(1)
    scale = HEAD_DIM ** -0.5 * math.log2(math.e)
    qk_gains = jnp.concatenate(
        [jnp.broadcast_to((at_q_norm.astype(F32) * scale)[:, None, within], (at_q_norm.shape[0], ATT_HEADS, HEAD_DIM)),
         jnp.broadcast_to(at_k_norm.astype(F32)[:, None, within], (at_k_norm.shape[0], ATT_KV_HEADS, HEAD_DIM))],
        axis=1)
    return dict(
        dn_w_in_main=dn_w_in[:, :, :qkvz].astype(BF16),
        dn_w_in_ba=dn_w_in[:, :, qkvz:].astype(BF16),
        dn_w_out=dn_w_out.astype(BF16),
        at_w_in=at_w_in[:, :, perm].astype(BF16),
        qk_gains=qk_gains,
        at_w_out=at_w_out.astype(BF16),
        mlp_w_up=mlp_w_up.astype(BF16),
        mlp_w_down=mlp_w_down.astype(BF16),
        ple_w_gate=ple_w_gate.astype(BF16),
        ple_w_proj=ple_w_proj.astype(BF16),
    )


def _trunk(x, p, wts, norm_mix, norm_mlp, dn_conv_w, dn_a_log, dn_dt_bias, dn_out_norm, ple_norm):
    batch, seq, d = x.shape
    depth = p.shape[0]
    cos2, sin2 = _rope_tables(seq)
    h = x.reshape(batch * seq, d)
    for i in range(depth):
        j = i // 2
        if i % 2 == 0:
            h = _deltanet_layer(h, batch, seq, norm_mix[i], wts['dn_w_in_main'][j], wts['dn_w_in_ba'][j],
                                dn_conv_w[j], dn_a_log[j], dn_dt_bias[j], dn_out_norm[j], wts['dn_w_out'][j])
        else:
            h = _attention_layer(h, batch, seq, norm_mix[i], wts['at_w_in'][j], wts['qk_gains'][j],
                                 wts['at_w_out'][j], cos2, sin2)
        h = mlp(h, norm_mlp[i], wts['mlp_w_up'][i], wts['mlp_w_down'][i])
        h = ple(h, ple_norm[i], wts['ple_w_gate'][i], p[i].reshape(batch * seq, -1), wts['ple_w_proj'][i])
    return h.reshape(batch, seq, d)


def kernel(x_prompt, x_sample, p_prompt, p_sample, norm_mix, norm_mlp, dn_w_in, dn_conv, dn_a_log, dn_dt_bias, dn_out_norm, dn_w_out, at_w_in, at_q_norm, at_k_norm, at_w_out, mlp_w_up, mlp_w_down, ple_norm, ple_w_gate, ple_w_proj):
    wts = _prepare_weights(dn_w_in, dn_w_out, at_w_in, at_q_norm, at_k_norm, at_w_out, mlp_w_up, mlp_w_down,
                           ple_w_gate, ple_w_proj)
    shared = (wts, norm_mix, norm_mlp, dn_conv, dn_a_log, dn_dt_bias, dn_out_norm, ple_norm)
    return (_trunk(x_prompt, p_prompt, *shared), _trunk(x_sample, p_sample, *shared))
```
